```python
import jax, jax.numpy as jnp
from jax import lax
import numpy as np

D_MODEL = 1024
BATCH = 8
SEQ = 2048
DEPTH = 4
DEC_BATCH = 128
DEC_SEQ = 1
PAST_LEN = 2048
PAGE_SIZE = 128

N_A_LAYERS = DEPTH // 2
N_B_LAYERS = DEPTH - N_A_LAYERS
N_DENSE_LAYERS = (DEPTH + 1) // 2
N_MOE_LAYERS = DEPTH // 2

CHUNK = 128
D_A = D_MODEL
A_GROUPS = 8
A_GROUP_DIM = D_A // A_GROUPS

N_HEADS = 16
HEAD_DIM = D_MODEL // N_HEADS
N_KV_GROUPS = 4
Q_PER_GROUP = N_HEADS // N_KV_GROUPS
CMP_LEN = 32
CMP_STRIDE = 16
CMP_HIDDEN = HEAD_DIM
SLC_BLOCK = 64
N_TOP = 16
WINDOW = 512
Q_BLOCK = 64
N_BRANCH_KV = 6
N_PAGED_KV = 4
N_WIN_KV = 2

D_FF = 2816
N_EXPERTS = 8
TOP_K = 2
MOE_ROWS = 128

RMS_EPS = 1e-6
LN_EPS = 1e-5
NEG_INF = -1e30
FORCE_SCORE = 1e4
F32 = jnp.float32

kernel_name = 'yoco_gmlp_nsa_hybrid_step'


def rmsnorm(x, g):
    xf = x.astype(F32)
    y = xf * lax.rsqrt(jnp.mean(xf * xf, axis=-1, keepdims=True) + RMS_EPS)
    return (y * g.astype(F32)).astype(x.dtype)


def layernorm(x, g, b):
    xf = x.astype(F32)
    mu = jnp.mean(xf, axis=-1, keepdims=True)
    var = jnp.mean(jnp.square(xf - mu), axis=-1, keepdims=True)
    y = (xf - mu) * lax.rsqrt(var + LN_EPS)
    return (y * g.astype(F32) + b.astype(F32)).astype(x.dtype)


def masked_softmax(s, mask):
    s = jnp.where(mask, s.astype(F32), NEG_INF)
    return jnp.where(mask, jax.nn.softmax(s, axis=-1), 0.0)


def alibi_slopes():
    h = np.arange(1, N_HEADS + 1, dtype=np.float32)
    return jnp.asarray(2.0 ** (-8.0 * h / N_HEADS), dtype=F32).reshape(N_KV_GROUPS, Q_PER_GROUP)


def swiglu(x, wg, wu, wd):
    return (jax.nn.silu(x @ wg) * (x @ wu)) @ wd


def moe_swiglu(x, w_router, b_router, w_gate, w_up, w_down):
    bsz, L, d = x.shape
    xt = x.reshape(-1, d)
    n = xt.shape[0]
    logits = (xt @ w_router + b_router).astype(F32)
    top_val, top_idx = lax.top_k(logits, TOP_K)
    gate = jax.nn.softmax(top_val, axis=-1)
    a = n * TOP_K
    expert = top_idx.reshape(a)
    token = jnp.arange(a, dtype=jnp.int32) // TOP_K
    order = jnp.argsort(expert)
    e_sorted = expert[order]
    counts = jnp.bincount(expert, length=N_EXPERTS)
    padded = (counts + MOE_ROWS - 1) // MOE_ROWS * MOE_ROWS
    pad_end = jnp.cumsum(padded)
    pad_start = pad_end - padded
    start = jnp.cumsum(counts) - counts
    dest = pad_start[e_sorted] + jnp.arange(a, dtype=jnp.int32) - start[e_sorted]
    n_rows = (-(-a // MOE_ROWS) + N_EXPERTS) * MOE_ROWS
    row_token = jnp.full((n_rows,), n, jnp.int32).at[dest].set(token[order])
    row_gate = jnp.zeros((n_rows,), F32).at[dest].set(gate.reshape(a)[order])
    x_rows = jnp.concatenate([xt, jnp.zeros((1, d), xt.dtype)], axis=0)[row_token]
    x_rows = x_rows.reshape(n_rows // MOE_ROWS, MOE_ROWS, d)
    blk_start = jnp.arange(n_rows // MOE_ROWS, dtype=jnp.int32) * MOE_ROWS
    blk_expert = jnp.minimum(jnp.sum(pad_end[None, :] <= blk_start[:, None], axis=1), N_EXPERTS - 1)

    def expert_block(args):
        xb, e = args
        return swiglu(xb, w_gate[e], w_up[e], w_down[e])

    y_rows = lax.map(expert_block, (x_rows, blk_expert)).reshape(n_rows, d)
    y = jnp.zeros((n + 1, d), F32).at[row_token].add(y_rows.astype(F32) * row_gate[:, None])
    return y[:n].astype(x.dtype).reshape(bsz, L, d)


def chunk_gmlp(h, w_in, ln_g, ln_b, w_s, b_s, w_out):
    bsz, L, _ = h.shape
    z = jax.nn.gelu(h @ w_in)
    u, v = jnp.split(z, 2, axis=-1)
    v = layernorm(v, ln_g, ln_b)
    n_chunks = -(-L // CHUNK)
    vc = jnp.pad(v, ((0, 0), (0, n_chunks * CHUNK - L), (0, 0)))
    vc = vc.reshape(bsz, n_chunks, CHUNK, A_GROUPS, A_GROUP_DIM)
    causal = jnp.tril(jnp.ones((CHUNK, CHUNK), w_s.dtype))
    mixed = jnp.einsum('gts,bcsge->bctge', w_s * causal, vc) + b_s.T[:, :, None]
    mixed = mixed.reshape(bsz, n_chunks * CHUNK, D_A)[:, :L]
    out = (u * mixed) @ w_out
    last = ((L - 1) // CHUNK) * CHUNK
    return out, v[:, last:]


def compress_blocks(x_raw, pe, w1, w2):
    Lk = x_raw.shape[1]
    n_cmp = (Lk - CMP_LEN) // CMP_STRIDE + 1
    idx = np.arange(n_cmp)[:, None] * CMP_STRIDE + np.arange(CMP_LEN)[None, :]
    blocks = x_raw[:, idx] + pe[None, None, :, None, :]
    hid = jax.nn.gelu(jnp.einsum('bnlgd,lde->bnge', blocks, w1))
    return jnp.einsum('bnge,ed->bngd', hid, w2)


def cmp_to_sel_overlap(n_cmp, n_sel):
    c0 = np.arange(n_cmp)[:, None] * CMP_STRIDE
    s0 = np.arange(n_sel)[None, :] * SLC_BLOCK
    return jnp.asarray(((c0 < s0 + SLC_BLOCK) & (c0 + CMP_LEN > s0)).astype(np.float32))


def build_nsa_shared(rows, win_rows, cmp_pe, cmp_w1, cmp_w2):
    bsz, Lk = rows.shape[:2]
    kc = compress_blocks(rows[:, :, 0], cmp_pe[0], cmp_w1[0], cmp_w2[0])
    vc = compress_blocks(rows[:, :, 1], cmp_pe[1], cmp_w1[1], cmp_w2[1])
    n_sel = -(-Lk // SLC_BLOCK)
    slc = jnp.pad(rows[:, :, 2:4], ((0, 0), (0, n_sel * SLC_BLOCK - Lk), (0, 0), (0, 0), (0, 0)))
    slc = slc.reshape(bsz, n_sel, SLC_BLOCK, 2, N_KV_GROUPS, HEAD_DIM).transpose(3, 0, 4, 1, 2, 5)
    win = jnp.pad(win_rows, ((0, 0), (WINDOW, Q_BLOCK), (0, 0), (0, 0), (0, 0)))
    return (kc, vc, slc[0], slc[1], win[:, :, 0], win[:, :, 1])


def nsa_layer(h, q_pos0, w_pos0, shared, w_in, b_gate, w_out):
    kc, vc, ks, vs, kw, vw = shared
    bsz, Lq, _ = h.shape
    proj = h @ w_in
    q = proj[..., :N_HEADS * HEAD_DIM].reshape(bsz, Lq, N_KV_GROUPS, Q_PER_GROUP, HEAD_DIM) * (HEAD_DIM ** -0.5)
    gates = jax.nn.sigmoid(proj[..., N_HEADS * HEAD_DIM:] + b_gate).reshape(bsz, Lq, N_KV_GROUPS, Q_PER_GROUP, 3)
    qb = min(Q_BLOCK, Lq)
    nqb = -(-Lq // qb)
    pad = nqb * qb - Lq

    def to_blocks(a):
        a = jnp.pad(a, ((0, 0), (0, pad)) + ((0, 0),) * (a.ndim - 2))
        return a.reshape((bsz, nqb, qb) + a.shape[2:]).swapaxes(0, 1)

    slopes = alibi_slopes()
    n_cmp = kc.shape[1]
    n_sel = ks.shape[2]
    n_top = min(N_TOP, n_sel)
    cmp_end = jnp.asarray(np.arange(n_cmp) * CMP_STRIDE + CMP_LEN - 1, jnp.int32)
    overlap = cmp_to_sel_overlap(n_cmp, n_sel)
    sel_ids = jnp.arange(n_sel, dtype=jnp.int32)
    b_ix = jnp.arange(bsz)[:, None, None, None]
    g_ix = jnp.arange(N_KV_GROUPS)[None, :, None, None]
    wb = WINDOW + qb - 1

    def attend_block(args):
        qblk, gblk, j0 = args
        t = q_pos0 + j0 + jnp.arange(qb, dtype=jnp.int32)
        s_c = jnp.einsum('bqgrd,bngd->bgrqn', qblk, kc).astype(F32)
        d_c = t[:, None] - cmp_end[None, :]
        p_c = masked_softmax(s_c - slopes[:, :, None, None] * d_c, d_c >= 0)
        o_c = jnp.einsum('bgrqn,bngd->bqgrd', p_c.astype(vc.dtype), vc)
        imp = jnp.einsum('bgrqn,nj->bgqj', p_c, overlap)
        cur = t // SLC_BLOCK
        valid = sel_ids[None, :] <= cur[:, None]
        forced = (sel_ids[None, :] == 0) | (sel_ids[None, :] == cur[:, None]) | (sel_ids[None, :] == cur[:, None] - 1)
        score = jnp.where(valid, jnp.where(forced, FORCE_SCORE, imp), NEG_INF)
        _, top = lax.top_k(score, n_top)
        k_sel = ks[b_ix, g_ix, top]
        v_sel = vs[b_ix, g_ix, top].reshape(bsz, N_KV_GROUPS, qb, n_top * SLC_BLOCK, HEAD_DIM)
        pos_s = top[..., None] * SLC_BLOCK + jnp.arange(SLC_BLOCK, dtype=jnp.int32)
        d_s = (t[:, None, None] - pos_s).reshape(bsz, N_KV_GROUPS, qb, n_top * SLC_BLOCK)[:, :, None]
        s_s = jnp.einsum('bqgrd,bgqnkd->bgrqnk', qblk, k_sel).astype(F32)
        s_s = s_s.reshape(bsz, N_KV_GROUPS, Q_PER_GROUP, qb, n_top * SLC_BLOCK)
        p_s = masked_softmax(s_s - slopes[None, :, :, None, None] * d_s, d_s >= 0)
        o_s = jnp.einsum('bgrqk,bgqkd->bqgrd', p_s.astype(v_sel.dtype), v_sel)
        k0 = q_pos0 + j0 - WINDOW + 1
        start = k0 - w_pos0 + WINDOW
        kwb = lax.dynamic_slice_in_dim(kw, start, wb, axis=1)
        vwb = lax.dynamic_slice_in_dim(vw, start, wb, axis=1)
        pos_w = k0 + jnp.arange(wb, dtype=jnp.int32)
        d_w = t[:, None] - pos_w[None, :]
        m_w = (d_w >= 0) & (d_w < WINDOW) & (pos_w >= w_pos0)[None, :]
        s_w = jnp.einsum('bqgrd,bkgd->bgrqk', qblk, kwb).astype(F32)
        p_w = masked_softmax(s_w - slopes[:, :, None, None] * d_w, m_w)
        o_w = jnp.einsum('bgrqk,bkgd->bqgrd', p_w.astype(vwb.dtype), vwb)
        return gblk[..., 0:1] * o_c + gblk[..., 1:2] * o_s + gblk[..., 2:3] * o_w

    j0s = jnp.arange(nqb, dtype=jnp.int32) * qb
    out = lax.map(attend_block, (to_blocks(q), to_blocks(gates), j0s))
    out = out.swapaxes(0, 1).reshape(bsz, nqb * qb, N_HEADS * HEAD_DIM)[:, :Lq]
    return out @ w_out


def run_trunk(x, q_pos0, past_rows, past_win,
              norm_mix, norm_ffn, norm_final,
              a_w_in, a_ln_g, a_ln_b, a_w_s, a_b_s, a_w_out,
              kv_norm, w_kv, cmp_pe, cmp_w1, cmp_w2,
              b_w_in, b_b_gate, b_w_out,
              f_w_gate, f_w_up, f_w_down,
              m_w_router, m_b_router, m_w_gate, m_w_up, m_w_down):
    bsz, L = x.shape[:2]
    a_states = []
    shared = None
    w_pos0 = 0
    for layer in range(DEPTH):
        if layer == N_A_LAYERS:
            kv = (rmsnorm(x, kv_norm) @ w_kv).reshape(bsz, L, N_BRANCH_KV, N_KV_GROUPS, HEAD_DIM)
            new_rows = kv[:, :, :N_PAGED_KV]
            new_win = kv[:, :, N_PAGED_KV:]
            if past_rows is None:
                rows, win = new_rows, new_win
                w_pos0 = 0
            else:
                rows = jnp.concatenate([past_rows.astype(new_rows.dtype), new_rows], axis=1)
                win = jnp.concatenate([past_win.astype(new_win.dtype), new_win], axis=1)
                w_pos0 = q_pos0 - past_win.shape[1]
            win_state = win[:, win.shape[1] - min(WINDOW, win.shape[1]):]
            shared = build_nsa_shared(rows, win, cmp_pe, cmp_w1, cmp_w2)
        h = rmsnorm(x, norm_mix[layer])
        if layer < N_A_LAYERS:
            mix, v_rows = chunk_gmlp(h, a_w_in[layer], a_ln_g[layer], a_ln_b[layer], a_w_s[layer], a_b_s[layer], a_w_out[layer])
            a_states.append(v_rows)
        else:
            i = layer - N_A_LAYERS
            mix = nsa_layer(h, q_pos0, w_pos0, shared, b_w_in[i], b_b_gate[i], b_w_out[i])
        x = x + mix
        h = rmsnorm(x, norm_ffn[layer])
        j = layer // 2
        if layer % 2 == 0:
            x = x + swiglu(h, f_w_gate[j], f_w_up[j], f_w_down[j])
        else:
            x = x + moe_swiglu(h, m_w_router[j], m_b_router[j], m_w_gate[j], m_w_up[j], m_w_down[j])
    return rmsnorm(x, norm_final), new_rows, win_state, jnp.stack(a_states)


def setup_inputs(seed: int = 0) -> dict:
    key = jax.random.key(seed)
    keys = iter(jax.random.split(key, 40))

    def nrm(shape, scale):
        return jax.random.normal(next(keys), shape, F32) * scale

    n_pages = PAST_LEN // PAGE_SIZE
    n_used = DEC_BATCH * n_pages
    n_phys = n_used + (n_used + 3) // 4
    page_table = jax.random.permutation(next(keys), n_phys)[:n_used].reshape(DEC_BATCH, n_pages).astype(jnp.int32)
    gate_cols = N_HEADS * HEAD_DIM + 3 * N_HEADS
    return {
        'x_prompt': nrm((BATCH, SEQ, D_MODEL), 1.0),
        'x_sample': nrm((DEC_BATCH, DEC_SEQ, D_MODEL), 1.0),
        'cache_kv': nrm((n_phys, PAGE_SIZE, N_PAGED_KV, N_KV_GROUPS, HEAD_DIM), 1.0),
        'state_win_kv': nrm((DEC_BATCH, min(WINDOW, PAST_LEN), N_WIN_KV, N_KV_GROUPS, HEAD_DIM), 1.0),
        'page_table': page_table,
        'norm_mix': 1.0 + nrm((DEPTH, D_MODEL), 0.1),
        'norm_ffn': 1.0 + nrm((DEPTH, D_MODEL), 0.1),
        'norm_final': 1.0 + nrm((D_MODEL,), 0.1),
        'a_w_in': nrm((N_A_LAYERS, D_MODEL, 2 * D_A), D_MODEL ** -0.5),
        'a_ln_g': 1.0 + nrm((N_A_LAYERS, D_A), 0.1),
        'a_ln_b': nrm((N_A_LAYERS, D_A), 0.02),
        'a_w_s': nrm((N_A_LAYERS, A_GROUPS, CHUNK, CHUNK), CHUNK ** -0.5),
        'a_b_s': 1.0 + nrm((N_A_LAYERS, A_GROUPS, CHUNK), 0.1),
        'a_w_out': nrm((N_A_LAYERS, D_A, D_MODEL), D_A ** -0.5),
        'kv_norm': 1.0 + nrm((D_MODEL,), 0.1),
        'w_kv': nrm((D_MODEL, N_BRANCH_KV * N_KV_GROUPS * HEAD_DIM), D_MODEL ** -0.5),
        'cmp_pe': nrm((2, CMP_LEN, HEAD_DIM), 0.1),
        'cmp_w1': nrm((2, CMP_LEN, HEAD_DIM, CMP_HIDDEN), (CMP_LEN * HEAD_DIM) ** -0.5),
        'cmp_w2': nrm((2, CMP_HIDDEN, HEAD_DIM), CMP_HIDDEN ** -0.5),
        'b_w_in': nrm((N_B_LAYERS, D_MODEL, gate_cols), D_MODEL ** -0.5),
        'b_b_gate': nrm((N_B_LAYERS, 3 * N_HEADS), 0.1),
        'b_w_out': nrm((N_B_LAYERS, N_HEADS * HEAD_DIM, D_MODEL), (N_HEADS * HEAD_DIM) ** -0.5),
        'f_w_gate': nrm((N_DENSE_LAYERS, D_MODEL, D_FF), D_MODEL ** -0.5),
        'f_w_up': nrm((N_DENSE_LAYERS, D_MODEL, D_FF), D_MODEL ** -0.5),
        'f_w_down': nrm((N_DENSE_LAYERS, D_FF, D_MODEL), D_FF ** -0.5),
        'm_w_router': nrm((N_MOE_LAYERS, D_MODEL, N_EXPERTS), D_MODEL ** -0.5),
        'm_b_router': nrm((N_MOE_LAYERS, N_EXPERTS), 0.01),
        'm_w_gate': nrm((N_MOE_LAYERS, N_EXPERTS, D_MODEL, D_FF), D_MODEL ** -0.5),
        'm_w_up': nrm((N_MOE_LAYERS, N_EXPERTS, D_MODEL, D_FF), D_MODEL ** -0.5),
        'm_w_down': nrm((N_MOE_LAYERS, N_EXPERTS, D_FF, D_MODEL), D_FF ** -0.5),
    }


def reference(x_prompt, x_sample, cache_kv, state_win_kv, page_table,
              norm_mix, norm_ffn, norm_final,
              a_w_in, a_ln_g, a_ln_b, a_w_s, a_b_s, a_w_out,
              kv_norm, w_kv, cmp_pe, cmp_w1, cmp_w2,
              b_w_in, b_b_gate, b_w_out,
              f_w_gate, f_w_up, f_w_down,
              m_w_router, m_b_router, m_w_gate, m_w_up, m_w_down):
    weights = (norm_mix, norm_ffn, norm_final,
               a_w_in, a_ln_g, a_ln_b, a_w_s, a_b_s, a_w_out,
               kv_norm, w_kv, cmp_pe, cmp_w1, cmp_w2,
               b_w_in, b_b_gate, b_w_out,
               f_w_gate, f_w_up, f_w_down,
               m_w_router, m_b_router, m_w_gate, m_w_up, m_w_down)
    y_prompt, kv_rows_prompt, win_prompt, gmlp_v_prompt = run_trunk(x_prompt, 0, None, None, *weights)
    n_pages = page_table.shape[1]
    past_rows = cache_kv[page_table].reshape(x_sample.shape[0], n_pages * cache_kv.shape[1], N_PAGED_KV, N_KV_GROUPS, HEAD_DIM)
    y_sample, kv_rows_sample, win_sample, gmlp_v_sample = run_trunk(x_sample, past_rows.shape[1], past_rows, state_win_kv, *weights)
    return (y_prompt, y_sample, kv_rows_prompt, kv_rows_sample, win_prompt, win_sample, gmlp_v_prompt, gmlp_v_sample)
```

```python
import functools

import numpy as np
import jax
import jax.numpy as jnp
from jax import lax
from jax.experimental import pallas as pl
from jax.experimental.pallas import tpu as pltpu

F32 = jnp.float32
BF16 = jnp.bfloat16
HIGHEST = lax.Precision.HIGHEST

D_MODEL = 1024
BATCH = 8
SEQ = 2048
DEPTH = 4
DEC_BATCH = 128
PAST_LEN = 2048
PAGE_SIZE = 128
N_A_LAYERS = DEPTH // 2
CHUNK = 128
D_A = D_MODEL
A_GROUPS = 8
N_HEADS = 16
HEAD_DIM = 64
N_KV_GROUPS = 4
Q_PER_GROUP = 4
CMP_LEN = 32
CMP_STRIDE = 16
SLC_BLOCK = 64
N_TOP = 16
WINDOW = 512
D_FF = 2816
N_EXPERTS = 8
TOP_K = 2
RMS_EPS = 1e-6
LN_EPS = 1e-5
NEG_INF = -1e30
FORCE_SCORE = 1e4

LANES = 128
TM = 512
N_P = BATCH * SEQ
N_PT = N_P // TM
N_TOT = N_P + TM
N_TILES = N_TOT // TM
N_REAL = N_P + DEC_BATCH
TQ = 128
BK_SEL = 256
BK_WIN = 128
N_CMP_PAD = 128
N_SEL_P = SEQ // SLC_BLOCK
N_PAGES = PAST_LEN // PAGE_SIZE
KV_ROW = 4 * N_KV_GROUPS * HEAD_DIM
T_MOE = 512
N_ASSIGN = N_REAL * TOP_K
N_MOE_BLOCKS = -(-N_ASSIGN // T_MOE) + N_EXPERTS
N_MOE_ROWS = N_MOE_BLOCKS * T_MOE
FF_SPLIT = 2
VMEM_LIMIT = 56 * 1024 * 1024

_SLOPES = [float(v) for v in
           (2.0 ** (-8.0 * np.arange(1, N_HEADS + 1, dtype=np.float32) / N_HEADS)).astype(np.float32)]


def _cparams(n_axes):
    return pltpu.CompilerParams(dimension_semantics=("arbitrary",) * n_axes,
                                vmem_limit_bytes=VMEM_LIMIT)


def _const(shape):
    nd = len(shape)
    return pl.BlockSpec(shape, lambda *_: (0,) * nd, pipeline_mode=pl.Buffered(1))


def _rms(x, g):
    return x * lax.rsqrt(jnp.mean(x * x, axis=-1, keepdims=True) + RMS_EPS) * g


def _dot(a, b, **kw):
    return jnp.dot(a, b, preferred_element_type=F32, **kw)


def _dot_nt(a, b, **kw):
    return lax.dot_general(a, b, (((1,), (1,)), ((), ())), preferred_element_type=F32, **kw)


def _a_mixer_kernel(x_ref, nrm_ref, win_ref, lng_ref, lnb_ref, wmix_ref, bias_ref, wout_ref,
                    xo_ref, vp_ref, vs_ref, mixed_ref):
    i = pl.program_id(0)
    x = x_ref[...]
    h = _rms(x, nrm_ref[...]).astype(BF16)
    z = jax.nn.gelu(_dot(h, win_ref[...]))
    u = z[:, :D_A]
    v = z[:, D_A:]
    mu = jnp.mean(v, axis=-1, keepdims=True)
    var = jnp.mean(jnp.square(v - mu), axis=-1, keepdims=True)
    v = (v - mu) * lax.rsqrt(var + LN_EPS) * lng_ref[...] + lnb_ref[...]

    @pl.when((i < N_PT) & (i % (SEQ // TM) == SEQ // TM - 1))
    def _():
        vp_ref[0] = v[TM - CHUNK:, :]

    @pl.when(i == N_PT)
    def _():
        vs_ref[...] = v[:DEC_BATCH, :]

    vb = v.astype(BF16)
    for c in range(TM // CHUNK):
        for g in range(A_GROUPS):
            cols = slice(g * LANES, (g + 1) * LANES)
            rows = slice(c * CHUNK, (c + 1) * CHUNK)
            mixed_ref[rows, cols] = _dot(wmix_ref[0, g], vb[rows, cols]) + bias_ref[0, :, cols]
    t = (u * mixed_ref[...]).astype(BF16)
    xo_ref[...] = x + _dot(t, wout_ref[...])


def _a_mixer(x, nrm, w_in, ln_g, ln_b, wmix, bias, w_out):
    return pl.pallas_call(
        _a_mixer_kernel,
        grid=(N_TILES,),
        in_specs=[
            pl.BlockSpec((TM, D_MODEL), lambda i: (i, 0)),
            _const((1, D_MODEL)),
            _const((D_MODEL, 2 * D_A)),
            _const((1, D_A)),
            _const((1, D_A)),
            pl.BlockSpec((1, A_GROUPS, CHUNK, CHUNK), lambda i: (i // N_PT, 0, 0, 0)),
            pl.BlockSpec((1, CHUNK, D_A), lambda i: (i // N_PT, 0, 0)),
            _const((D_A, D_MODEL)),
        ],
        out_specs=[
            pl.BlockSpec((TM, D_MODEL), lambda i: (i, 0)),
            pl.BlockSpec((1, CHUNK, D_A), lambda i: (jnp.minimum(i // (SEQ // TM), BATCH - 1), 0, 0)),
            pl.BlockSpec((DEC_BATCH, D_A), lambda i: (0, 0)),
        ],
        out_shape=[
            jax.ShapeDtypeStruct((N_TOT, D_MODEL), F32),
            jax.ShapeDtypeStruct((BATCH, CHUNK, D_A), F32),
            jax.ShapeDtypeStruct((DEC_BATCH, D_A), F32),
        ],
        scratch_shapes=[pltpu.VMEM((TM, D_A), F32)],
        compiler_params=_cparams(1),
    )(x, nrm, w_in, ln_g, ln_b, wmix, bias, w_out)


def _swiglu_block(h, wg_ref, wu_ref, wd_ref, lead):
    ffh = D_FF // FF_SPLIT
    out = None
    for s in range(FF_SPLIT):
        cols = slice(s * ffh, (s + 1) * ffh)
        g = _dot(h, wg_ref[lead + (slice(None), cols)])
        u = _dot(h, wu_ref[lead + (slice(None), cols)])
        a = (jax.nn.silu(g) * u).astype(BF16)
        part = _dot(a, wd_ref[lead + (cols, slice(None))])
        out = part if out is None else out + part
    return out


def _ffn_kernel(x_ref, nrm_ref, wg_ref, wu_ref, wd_ref, o_ref):
    x = x_ref[...]
    h = _rms(x, nrm_ref[...]).astype(BF16)
    o_ref[...] = x + _swiglu_block(h, wg_ref, wu_ref, wd_ref, ())


def _ffn(x, nrm, wg, wu, wd):
    return pl.pallas_call(
        _ffn_kernel,
        grid=(N_TILES,),
        in_specs=[
            pl.BlockSpec((TM, D_MODEL), lambda i: (i, 0)),
            _const((1, D_MODEL)),
            _const((D_MODEL, D_FF)),
            _const((D_MODEL, D_FF)),
            _const((D_FF, D_MODEL)),
        ],
        out_specs=pl.BlockSpec((TM, D_MODEL), lambda i: (i, 0)),
        out_shape=jax.ShapeDtypeStruct((N_TOT, D_MODEL), F32),
        compiler_params=_cparams(1),
    )(x, nrm, wg, wu, wd)


def _router_kernel(x_ref, nrm_ref, wr_ref, br_ref, h_ref, r_ref):
    h = _rms(x_ref[...], nrm_ref[...])
    h_ref[...] = h.astype(BF16)
    logits = _dot(h, wr_ref[...], precision=HIGHEST) + br_ref[...]
    lane = lax.broadcasted_iota(jnp.int32, logits.shape, 1).astype(F32)
    big = float(LANES)
    m1 = jnp.max(logits, axis=1, keepdims=True)
    i1 = jnp.min(jnp.where(logits == m1, lane, big), axis=1, keepdims=True)
    l2 = jnp.where(lane == i1, -jnp.inf, logits)
    m2 = jnp.max(l2, axis=1, keepdims=True)
    i2 = jnp.min(jnp.where(l2 == m2, lane, big), axis=1, keepdims=True)
    e = jnp.exp(m2 - m1)
    g1 = 1.0 / (1.0 + e)
    g2 = e / (1.0 + e)
    r_ref[...] = jnp.where(lane == 0.0, i1, jnp.where(lane == 1.0, i2,
                           jnp.where(lane == 2.0, g1, jnp.where(lane == 3.0, g2, 0.0))))


def _router(x, nrm, wr, br):
    return pl.pallas_call(
        _router_kernel,
        grid=(N_TILES,),
        in_specs=[
            pl.BlockSpec((TM, D_MODEL), lambda i: (i, 0)),
            _const((1, D_MODEL)),
            _const((D_MODEL, LANES)),
            _const((1, LANES)),
        ],
        out_specs=[
            pl.BlockSpec((TM, D_MODEL), lambda i: (i, 0)),
            pl.BlockSpec((TM, LANES), lambda i: (i, 0)),
        ],
        out_shape=[
            jax.ShapeDtypeStruct((N_TOT, D_MODEL), BF16),
            jax.ShapeDtypeStruct((N_TOT, LANES), F32),
        ],
        compiler_params=_cparams(1),
    )(x, nrm, wr, br)


def _moe_kernel(be_ref, na_ref, x_ref, wg_ref, wu_ref, wd_ref, o_ref):
    i = pl.program_id(0)

    @pl.when(i < na_ref[0])
    def _():
        o_ref[...] = _swiglu_block(x_ref[...], wg_ref, wu_ref, wd_ref, (0,))

    @pl.when(i >= na_ref[0])
    def _():
        o_ref[...] = jnp.zeros(o_ref.shape, o_ref.dtype)


def _moe_experts(blk_expert, n_active, x_rows, wg, wu, wd):
    return pl.pallas_call(
        _moe_kernel,
        grid_spec=pltpu.PrefetchScalarGridSpec(
            num_scalar_prefetch=2,
            grid=(N_MOE_BLOCKS,),
            in_specs=[
                pl.BlockSpec((T_MOE, D_MODEL), lambda i, be, na: (i, 0)),
                pl.BlockSpec((1, D_MODEL, D_FF), lambda i, be, na: (be[i], 0, 0)),
                pl.BlockSpec((1, D_MODEL, D_FF), lambda i, be, na: (be[i], 0, 0)),
                pl.BlockSpec((1, D_FF, D_MODEL), lambda i, be, na: (be[i], 0, 0)),
            ],
            out_specs=pl.BlockSpec((T_MOE, D_MODEL), lambda i, be, na: (i, 0)),
        ),
        out_shape=jax.ShapeDtypeStruct((N_MOE_ROWS, D_MODEL), F32),
        compiler_params=_cparams(1),
    )(blk_expert, n_active, x_rows, wg, wu, wd)


def _moe_layer(x, nrm, wr, br, wg, wu, wd):
    h, r = _router(x, nrm, wr, br)
    r = r[:N_REAL]
    expert = r[:, 0:2].astype(jnp.int32).reshape(N_ASSIGN)
    gate = r[:, 2:4]
    onehot = (expert[:, None] == jnp.arange(N_EXPERTS, dtype=jnp.int32)[None, :]).astype(jnp.int32)
    csum = jnp.cumsum(onehot, axis=0)
    counts = csum[-1]
    rank = jnp.sum(csum * onehot, axis=1) - 1
    padded = (counts + T_MOE - 1) // T_MOE * T_MOE
    pad_end = jnp.cumsum(padded)
    pad_start = pad_end - padded
    dest = jnp.sum(pad_start[None, :] * onehot, axis=1) + rank
    blk_start = jnp.arange(N_MOE_BLOCKS, dtype=jnp.int32) * T_MOE
    blk_expert = jnp.minimum(jnp.sum(pad_end[None, :] <= blk_start[:, None], axis=1),
                             N_EXPERTS - 1).astype(jnp.int32)
    n_active = (pad_end[-1:] // T_MOE).astype(jnp.int32)
    x_rows = jnp.zeros((N_MOE_ROWS, D_MODEL), BF16).at[dest].set(jnp.repeat(h[:N_REAL], TOP_K, axis=0))
    y_rows = _moe_experts(blk_expert, n_active, x_rows, wg, wu, wd)
    d2 = dest.reshape(N_REAL, TOP_K)
    y = gate[:, 0:1] * y_rows[d2[:, 0]] + gate[:, 1:2] * y_rows[d2[:, 1]]
    return x.at[:N_REAL].add(y)


def _kvproj_kernel(x_ref, nrm_ref, wkv_ref, wvt_ref, rows_ref, rows_s_ref, win_ref,
                   ksel_ref, kwin_ref, vselt_ref, vwint_ref):
    i = pl.program_id(0)
    h = _rms(x_ref[...], nrm_ref[...]).astype(BF16)
    kv = _dot(h, wkv_ref[...])
    n_rows = N_KV_GROUPS * HEAD_DIM * 4

    @pl.when(i < N_PT)
    def _():
        rows_ref[...] = kv[:, :n_rows]

    @pl.when(i == N_PT)
    def _():
        rows_s_ref[...] = kv[:, :n_rows]

    win_ref[...] = kv[:, n_rows:]
    for g in range(N_KV_GROUPS):
        ksel_ref[g] = kv[:, 512 + g * HEAD_DIM:512 + (g + 1) * HEAD_DIM].astype(BF16)
        kwin_ref[g] = kv[:, 1024 + g * HEAD_DIM:1024 + (g + 1) * HEAD_DIM].astype(BF16)
    vt = _dot_nt(wvt_ref[...], h).astype(BF16)
    for g in range(N_KV_GROUPS):
        for j in range(TM // BK_SEL):
            vselt_ref[g, j] = vt[g * HEAD_DIM:(g + 1) * HEAD_DIM, j * BK_SEL:(j + 1) * BK_SEL]
        for j in range(TM // BK_WIN):
            vwint_ref[g, j] = vt[256 + g * HEAD_DIM:256 + (g + 1) * HEAD_DIM, j * BK_WIN:(j + 1) * BK_WIN]


def _kvproj(x, nrm, wkv, wvt):
    n_rows = 4 * N_KV_GROUPS * HEAD_DIM
    n_win = 2 * N_KV_GROUPS * HEAD_DIM
    return pl.pallas_call(
        _kvproj_kernel,
        grid=(N_TILES,),
        in_specs=[
            pl.BlockSpec((TM, D_MODEL), lambda i: (i, 0)),
            _const((1, D_MODEL)),
            _const((D_MODEL, n_rows + n_win)),
            _const((n_win, D_MODEL)),
        ],
        out_specs=[
            pl.BlockSpec((TM, n_rows), lambda i: (jnp.minimum(i, N_PT - 1), 0)),
            pl.BlockSpec((TM, n_rows), lambda i: (0, 0)),
            pl.BlockSpec((TM, n_win), lambda i: (i, 0)),
            pl.BlockSpec((N_KV_GROUPS, TM, HEAD_DIM), lambda i: (0, i, 0)),
            pl.BlockSpec((N_KV_GROUPS, TM, HEAD_DIM), lambda i: (0, i, 0)),
            pl.BlockSpec((N_KV_GROUPS, TM // BK_SEL, HEAD_DIM, BK_SEL), lambda i: (0, i, 0, 0)),
            pl.BlockSpec((N_KV_GROUPS, TM // BK_WIN, HEAD_DIM, BK_WIN), lambda i: (0, i, 0, 0)),
        ],
        out_shape=[
            jax.ShapeDtypeStruct((N_P, n_rows), F32),
            jax.ShapeDtypeStruct((TM, n_rows), F32),
            jax.ShapeDtypeStruct((N_TOT, n_win), F32),
            jax.ShapeDtypeStruct((N_KV_GROUPS, N_TOT, HEAD_DIM), BF16),
            jax.ShapeDtypeStruct((N_KV_GROUPS, N_TOT, HEAD_DIM), BF16),
            jax.ShapeDtypeStruct((N_KV_GROUPS, N_TOT // BK_SEL, HEAD_DIM, BK_SEL), BF16),
            jax.ShapeDtypeStruct((N_KV_GROUPS, N_TOT // BK_WIN, HEAD_DIM, BK_WIN), BF16),
        ],
        compiler_params=_cparams(1),
    )(x, nrm, wkv, wvt)


def _compress_body(pieces, ws_ref, c_ref, w2_ref, kc_ref, vct_ref):
    gd = N_KV_GROUPS * HEAD_DIM
    for kind in range(2):
        y = None
        for r in range(CMP_STRIDE):
            cols = slice(r * KV_ROW + kind * gd, r * KV_ROW + (kind + 1) * gd)
            segs = [ref[lead + (slice(None), cols)] for ref, lead in pieces]
            xr = segs[0] if len(segs) == 1 else jnp.concatenate(segs, axis=0)
            part = _dot(xr.astype(BF16), ws_ref[kind, r * gd:(r + 1) * gd, :])
            y = part if y is None else y + part
        a = y[:, :gd]
        b = pltpu.roll(y[:, gd:], N_CMP_PAD - 1, axis=0)
        hid = jax.nn.gelu(a + b + c_ref[kind])
        out = _dot(hid.astype(BF16), w2_ref[kind])
        if kind == 0:
            kc_ref[0] = out.astype(BF16)
        else:
            vct_ref[0] = out.T.astype(BF16)


def _compress_prompt_kernel(rows_ref, ws_ref, c_ref, w2_ref, kc_ref, vct_ref):
    _compress_body([(rows_ref, ())], ws_ref, c_ref, w2_ref, kc_ref, vct_ref)


def _compress_sample_kernel(pt_ref, *refs):
    pages = refs[:N_PAGES]
    ws_ref, c_ref, w2_ref, kc_ref, vct_ref = refs[N_PAGES:]
    _compress_body([(p, (0,)) for p in pages], ws_ref, c_ref, w2_ref, kc_ref, vct_ref)


_CMP_W_SHAPES = [(2, CMP_STRIDE * 256, 512), (2, 1, 256), (2, 256, 256)]


def _compress_prompt(rows_p, ws, cvec, w2bd):
    return pl.pallas_call(
        _compress_prompt_kernel,
        grid=(BATCH,),
        in_specs=[pl.BlockSpec((SEQ // CMP_STRIDE, CMP_STRIDE * KV_ROW), lambda b: (b, 0))]
        + [_const(s) for s in _CMP_W_SHAPES],
        out_specs=[
            pl.BlockSpec((1, N_CMP_PAD, 256), lambda b: (b, 0, 0)),
            pl.BlockSpec((1, 256, N_CMP_PAD), lambda b: (b, 0, 0)),
        ],
        out_shape=[
            jax.ShapeDtypeStruct((BATCH, N_CMP_PAD, 256), BF16),
            jax.ShapeDtypeStruct((BATCH, 256, N_CMP_PAD), BF16),
        ],
        compiler_params=_cparams(1),
    )(rows_p.reshape(N_P // CMP_STRIDE, CMP_STRIDE * KV_ROW), ws, cvec, w2bd)


def _page_spec(p, col_block):
    return pl.BlockSpec((1, PAGE_SIZE, 512), lambda b, pt: (pt[b, p], 0, col_block))


def _compress_sample(page_table, cache_seg, ws, cvec, w2bd):
    n_seg = PAGE_SIZE // CMP_STRIDE
    seg_spec = lambda p: pl.BlockSpec((1, n_seg, CMP_STRIDE * KV_ROW), lambda b, pt: (pt[b, p], 0, 0))
    wspecs = [pl.BlockSpec(s, functools.partial(lambda nd, b, pt: (0,) * nd, len(s)),
                           pipeline_mode=pl.Buffered(1)) for s in _CMP_W_SHAPES]
    return pl.pallas_call(
        _compress_sample_kernel,
        grid_spec=pltpu.PrefetchScalarGridSpec(
            num_scalar_prefetch=1,
            grid=(DEC_BATCH,),
            in_specs=[seg_spec(p) for p in range(N_PAGES)] + wspecs,
            out_specs=[
                pl.BlockSpec((1, N_CMP_PAD, 256), lambda b, pt: (b, 0, 0)),
                pl.BlockSpec((1, 256, N_CMP_PAD), lambda b, pt: (b, 0, 0)),
            ],
        ),
        out_shape=[
            jax.ShapeDtypeStruct((DEC_BATCH, N_CMP_PAD, 256), BF16),
            jax.ShapeDtypeStruct((DEC_BATCH, 256, N_CMP_PAD), BF16),
        ],
        compiler_params=_cparams(1),
    )(page_table, *([cache_seg] * N_PAGES), ws, cvec, w2bd)


def _qproj_kernel(x_ref, nrm_ref, w_ref, bg_ref, q_ref, g_ref):
    h = _rms(x_ref[...], nrm_ref[...]).astype(BF16)
    p = _dot(h, w_ref[...])
    q_ref[...] = p[:, :D_MODEL] * (HEAD_DIM ** -0.5)
    g_ref[...] = jax.nn.sigmoid(p[:, D_MODEL:] + bg_ref[...])


def _qproj(x, nrm, w, bg):
    return pl.pallas_call(
        _qproj_kernel,
        grid=(N_TILES,),
        in_specs=[
            pl.BlockSpec((TM, D_MODEL), lambda i: (i, 0)),
            _const((1, D_MODEL)),
            _const((D_MODEL, D_MODEL + LANES)),
            _const((1, LANES)),
        ],
        out_specs=[
            pl.BlockSpec((TM, D_MODEL), lambda i: (i, 0)),
            pl.BlockSpec((TM, LANES), lambda i: (i, 0)),
        ],
        out_shape=[
            jax.ShapeDtypeStruct((N_TOT, D_MODEL), F32),
            jax.ShapeDtypeStruct((N_TOT, LANES), F32),
        ],
        compiler_params=_cparams(1),
    )(x, nrm, w, bg)


def _top_mask(score, n_sel, idx, axis):
    cnt = jnp.zeros(score.shape, F32)
    for i in range(n_sel):
        row = score[i:i + 1, :] if axis == 0 else score[:, i:i + 1]
        beats = (row > score) | ((row == score) & (idx > i))
        cnt = cnt + jnp.where(beats, 1.0, 0.0)
    return jnp.where(cnt < float(N_TOP), 0.0, NEG_INF)


def _nsa_prompt_kernel(q_ref, gt_ref, x_ref, kc_ref, vct_ref, ks_ref, vst_ref, kw_ref, vwt_ref,
                       ovl_ref, wout_ref, o_ref, ot_ref, selneg_ref, m_ref, l_ref, acc_ref):
    qi = pl.program_id(1)
    t0 = qi * TQ
    qt_all = q_ref[...].T
    gt = gt_ref[...].T
    tq = t0 + lax.broadcasted_iota(jnp.int32, (1, TQ), 1)

    def flash(k_ref, vt_ref, g, qgt, c_lo, c_hi, bk, mask_fn):
        m_ref[...] = jnp.full(m_ref.shape, NEG_INF, F32)
        l_ref[...] = jnp.zeros(l_ref.shape, F32)
        acc_ref[...] = jnp.zeros(acc_ref.shape, F32)
        d0 = (lax.broadcasted_iota(jnp.int32, (bk, TQ), 1)
              - lax.broadcasted_iota(jnp.int32, (bk, TQ), 0))

        def body(c, carry):
            k = k_ref[g, pl.ds(pl.multiple_of(c * bk, bk), bk), :]
            st = _dot(k, qgt)
            d = d0 + (t0 - c * bk)
            madd = mask_fn(c, d)
            df = d.astype(F32)
            ps = []
            alphas = []
            for r in range(Q_PER_GROUP):
                lanes = slice(r * TQ, (r + 1) * TQ)
                s = st[:, lanes] - _SLOPES[g * Q_PER_GROUP + r] * df + madd
                m_old = m_ref[:, lanes]
                m_new = jnp.maximum(m_old, jnp.max(s, axis=0, keepdims=True))
                alpha = jnp.exp(m_old - m_new)
                p = jnp.exp(s - m_new)
                l_ref[:, lanes] = alpha * l_ref[:, lanes] + jnp.sum(p, axis=0, keepdims=True)
                m_ref[:, lanes] = m_new
                ps.append(p.astype(BF16))
                alphas.append(alpha)
            pt = jnp.concatenate(ps, axis=1)
            alpha_all = jnp.concatenate(alphas, axis=1)
            acc_ref[...] = acc_ref[...] * alpha_all + _dot(vt_ref[g, c], pt)
            return carry

        lax.fori_loop(c_lo, c_hi, body, 0)
        return acc_ref[...] / l_ref[...]

    def sel_mask(c, d):
        rows = [jnp.broadcast_to(selneg_ref[pl.ds(c * (BK_SEL // SLC_BLOCK) + jj, 1), :], (SLC_BLOCK, TQ))
                for jj in range(BK_SEL // SLC_BLOCK)]
        return jnp.where(d >= 0, jnp.concatenate(rows, axis=0), NEG_INF)

    def win_mask(c, d):
        return jnp.where((d >= 0) & (d < WINDOW), 0.0, NEG_INF)

    n_idx = lax.broadcasted_iota(jnp.int32, (N_CMP_PAD, 1), 0)
    d_c = tq - (n_idx * CMP_STRIDE + CMP_LEN - 1)
    ok_c = d_c >= 0
    d_cf = d_c.astype(F32)
    j_idx = lax.broadcasted_iota(jnp.int32, (N_SEL_P, 1), 0)
    cur = tq // SLC_BLOCK
    valid = j_idx <= cur
    forced = (j_idx == 0) | (j_idx == cur) | (j_idx == cur - 1)

    for g in range(N_KV_GROUPS):
        qgt = jnp.concatenate(
            [qt_all[(g * Q_PER_GROUP + r) * HEAD_DIM:(g * Q_PER_GROUP + r + 1) * HEAD_DIM, :]
             for r in range(Q_PER_GROUP)], axis=1).astype(BF16)

        st = _dot(kc_ref[0, :, g * HEAD_DIM:(g + 1) * HEAD_DIM], qgt)
        psum = jnp.zeros((N_CMP_PAD, TQ), F32)
        ps = []
        for r in range(Q_PER_GROUP):
            s = st[:, r * TQ:(r + 1) * TQ] - _SLOPES[g * Q_PER_GROUP + r] * d_cf
            s = jnp.where(ok_c, s, NEG_INF)
            p = jnp.exp(s - jnp.max(s, axis=0, keepdims=True))
            p = p / jnp.sum(p, axis=0, keepdims=True)
            p = jnp.where(ok_c, p, 0.0)
            psum = psum + p
            ps.append(p.astype(BF16))
        oc_t = _dot(vct_ref[0, g * HEAD_DIM:(g + 1) * HEAD_DIM, :], jnp.concatenate(ps, axis=1))

        imp = _dot(ovl_ref[...], psum, precision=HIGHEST)
        score = jnp.where(valid, jnp.where(forced, FORCE_SCORE, imp), NEG_INF)
        selneg_ref[...] = _top_mask(score, N_SEL_P, j_idx, 0)

        os_t = flash(ks_ref, vst_ref, g, qgt, 0, (t0 + TQ - 1) // BK_SEL + 1, BK_SEL, sel_mask)
        ow_t = flash(kw_ref, vwt_ref, g, qgt, jnp.maximum(qi - WINDOW // BK_WIN, 0), qi + 1, BK_WIN, win_mask)

        for r in range(Q_PER_GROUP):
            hd = g * Q_PER_GROUP + r
            lanes = slice(r * TQ, (r + 1) * TQ)
            ot_ref[hd * HEAD_DIM:(hd + 1) * HEAD_DIM, :] = (
                gt[3 * hd:3 * hd + 1, :] * oc_t[:, lanes]
                + gt[3 * hd + 1:3 * hd + 2, :] * os_t[:, lanes]
                + gt[3 * hd + 2:3 * hd + 3, :] * ow_t[:, lanes])

    o = ot_ref[...].T.astype(BF16)
    o_ref[...] = x_ref[...] + _dot(o, wout_ref[...])


def _nsa_prompt(q, gates, x, kc, vct, ksel, vselt, kwin, vwint, ovl_t, w_out):
    nq = SEQ // TQ
    tile = lambda b, qi: (b * nq + qi, 0)
    return pl.pallas_call(
        _nsa_prompt_kernel,
        grid=(BATCH, nq),
        in_specs=[
            pl.BlockSpec((TQ, D_MODEL), tile),
            pl.BlockSpec((TQ, LANES), tile),
            pl.BlockSpec((TQ, D_MODEL), tile),
            pl.BlockSpec((1, N_CMP_PAD, 256), lambda b, qi: (b, 0, 0)),
            pl.BlockSpec((1, 256, N_CMP_PAD), lambda b, qi: (b, 0, 0)),
            pl.BlockSpec((N_KV_GROUPS, SEQ, HEAD_DIM), lambda b, qi: (0, b, 0)),
            pl.BlockSpec((N_KV_GROUPS, SEQ // BK_SEL, HEAD_DIM, BK_SEL), lambda b, qi: (0, b, 0, 0)),
            pl.BlockSpec((N_KV_GROUPS, SEQ, HEAD_DIM), lambda b, qi: (0, b, 0)),
            pl.BlockSpec((N_KV_GROUPS, SEQ // BK_WIN, HEAD_DIM, BK_WIN), lambda b, qi: (0, b, 0, 0)),
            _const((N_SEL_P, N_CMP_PAD)),
            _const((D_MODEL, D_MODEL)),
        ],
        out_specs=pl.BlockSpec((TQ, D_MODEL), tile),
        out_shape=jax.ShapeDtypeStruct((N_TOT, D_MODEL), F32),
        input_output_aliases={2: 0},
        scratch_shapes=[
            pltpu.VMEM((D_MODEL, TQ), F32),
            pltpu.VMEM((N_SEL_P, TQ), F32),
            pltpu.VMEM((1, Q_PER_GROUP * TQ), F32),
            pltpu.VMEM((1, Q_PER_GROUP * TQ), F32),
            pltpu.VMEM((HEAD_DIM, Q_PER_GROUP * TQ), F32),
        ],
        compiler_params=_cparams(2),
    )(q, gates, x, kc, vct, ksel, vselt, kwin, vwint, ovl_t, w_out)


def _nsa_sample_kernel(pt_ref, q_ref, g_ref, kc_ref, vct_ref, kvn_ref, winn_ref, win_ref, *refs):
    pages = refs[:N_PAGES]
    slope_ref, ovl_ref, exp_ref, o_ref = refs[N_PAGES:]
    gd = N_KV_GROUPS * HEAD_DIM
    t = PAST_LEN
    q = q_ref[0]
    qh = jnp.concatenate([q[:, h * HEAD_DIM:(h + 1) * HEAD_DIM] for h in range(N_HEADS)], axis=0)
    q4 = jnp.concatenate([qh] * N_KV_GROUPS, axis=1)
    hrow = lax.broadcasted_iota(jnp.int32, (N_HEADS, gd), 0)
    col = lax.broadcasted_iota(jnp.int32, (N_HEADS, gd), 1)
    own = (col // HEAD_DIM) == (hrow // Q_PER_GROUP)
    qbd = jnp.where(own, q4, 0.0).astype(BF16)
    qbd_f = qbd.astype(F32)
    slope = slope_ref[:, 0:1]

    def new_key_score(k_new):
        return jnp.sum(qbd_f * k_new.astype(BF16).astype(F32), axis=1, keepdims=True)

    def new_val(p_new, v_new):
        return p_new.astype(BF16).astype(F32) * v_new.astype(BF16).astype(F32)

    lane_c = lax.broadcasted_iota(jnp.int32, (1, N_CMP_PAD), 1)
    d_c = t - (lane_c * CMP_STRIDE + CMP_LEN - 1)
    ok_c = d_c >= 0
    s = _dot_nt(qbd, kc_ref[0]) - slope * d_c.astype(F32)
    s = jnp.where(ok_c, s, NEG_INF)
    p = jnp.exp(s - jnp.max(s, axis=1, keepdims=True))
    p = p / jnp.sum(p, axis=1, keepdims=True)
    p = jnp.where(ok_c, p, 0.0)
    o_c = _dot_nt(p.astype(BF16), vct_ref[0])

    h16r = lax.broadcasted_iota(jnp.int32, (N_HEADS, N_HEADS), 0) // Q_PER_GROUP
    h16c = lax.broadcasted_iota(jnp.int32, (N_HEADS, N_HEADS), 1) // Q_PER_GROUP
    pg = _dot(jnp.where(h16r == h16c, 1.0, 0.0), p, precision=HIGHEST)
    imp = _dot(pg, ovl_ref[...], precision=HIGHEST)
    j_idx = lax.broadcasted_iota(jnp.int32, (1, LANES), 1)
    cur = t // SLC_BLOCK
    n_sel = cur + 1
    forced = (j_idx == 0) | (j_idx == cur) | (j_idx == cur - 1)
    score = jnp.where(j_idx <= cur, jnp.where(forced, FORCE_SCORE, imp), NEG_INF)
    selneg = _top_mask(score, n_sel, j_idx, 1)
    sel = jnp.where(selneg == 0.0, 1.0, 0.0).astype(BF16)
    selexp = _dot(sel, exp_ref[...])

    kvn = kvn_ref[0]
    s_all = jnp.concatenate([_dot_nt(qbd, pg_ref[0, :, 0:gd].astype(BF16)) for pg_ref in pages], axis=1)
    pos = lax.broadcasted_iota(jnp.int32, (1, PAST_LEN), 1)
    s_all = jnp.where(selexp > 0.5, s_all - slope * (t - pos).astype(F32), NEG_INF)
    s_new = new_key_score(kvn[:, 0:gd])
    m = jnp.maximum(jnp.max(s_all, axis=1, keepdims=True), s_new)
    p_all = jnp.exp(s_all - m)
    p_new = jnp.exp(s_new - m)
    l = jnp.sum(p_all, axis=1, keepdims=True) + p_new
    o_s = new_val(p_new, kvn[:, gd:])
    for i, pg_ref in enumerate(pages):
        o_s = o_s + _dot(p_all[:, i * PAGE_SIZE:(i + 1) * PAGE_SIZE].astype(BF16),
                         pg_ref[0, :, gd:].astype(BF16))
    o_s = o_s / l

    win = win_ref[0]
    winn = winn_ref[0]
    i_w = lax.broadcasted_iota(jnp.int32, (1, WINDOW), 1)
    s_w = _dot_nt(qbd, win[:, 0:gd].astype(BF16)) - slope * (WINDOW - i_w).astype(F32)
    s_w = jnp.where(i_w >= 1, s_w, NEG_INF)
    s_wn = new_key_score(winn[:, 0:gd])
    m = jnp.maximum(jnp.max(s_w, axis=1, keepdims=True), s_wn)
    p_w = jnp.exp(s_w - m)
    p_wn = jnp.exp(s_wn - m)
    l = jnp.sum(p_w, axis=1, keepdims=True) + p_wn
    o_w = (new_val(p_wn, winn[:, gd:]) + _dot(p_w.astype(BF16), win[:, gd:].astype(BF16))) / l

    grow = g_ref[0]
    h128 = lax.broadcasted_iota(jnp.int32, (N_HEADS, LANES), 0)
    c128 = lax.broadcasted_iota(jnp.int32, (N_HEADS, LANES), 1)

    def gate(br):
        return jnp.sum(jnp.where(c128 == 3 * h128 + br, grow, 0.0), axis=1, keepdims=True)

    o = jnp.where(own, gate(0) * o_c + gate(1) * o_s + gate(2) * o_w, 0.0)
    oh = (o[:, 0:HEAD_DIM] + o[:, HEAD_DIM:2 * HEAD_DIM]
          + o[:, 2 * HEAD_DIM:3 * HEAD_DIM] + o[:, 3 * HEAD_DIM:4 * HEAD_DIM])
    o_ref[0] = jnp.concatenate([oh[h:h + 1, :] for h in range(N_HEADS)], axis=1)


def _nsa_sample(page_table, q_s, g_s, kc, vct, kvn, winn, state_win, cache3, slopes, ovl, expand):
    per_b = lambda *shape: pl.BlockSpec((1,) + shape, lambda b, pt: (b,) + (0,) * len(shape))
    cst = lambda shape: pl.BlockSpec(shape, functools.partial(lambda nd, b, pt: (0,) * nd, len(shape)),
                                     pipeline_mode=pl.Buffered(1))
    return pl.pallas_call(
        _nsa_sample_kernel,
        grid_spec=pltpu.PrefetchScalarGridSpec(
            num_scalar_prefetch=1,
            grid=(DEC_BATCH,),
            in_specs=[
                per_b(1, D_MODEL), per_b(1, LANES), per_b(N_CMP_PAD, 256), per_b(256, N_CMP_PAD),
                per_b(1, 512), per_b(1, 512), per_b(WINDOW, 512),
            ] + [_page_spec(p, 1) for p in range(N_PAGES)] + [
                cst((N_HEADS, LANES)), cst((N_CMP_PAD, LANES)), cst((LANES, PAST_LEN)),
            ],
            out_specs=per_b(1, D_MODEL),
        ),
        out_shape=jax.ShapeDtypeStruct((DEC_BATCH, 1, D_MODEL), F32),
        compiler_params=_cparams(1),
    )(page_table, q_s, g_s, kc, vct, kvn, winn, state_win, *([cache3] * N_PAGES), slopes, ovl, expand)


def _outproj_sample_kernel(o_ref, x_ref, w_ref, xo_ref):
    xo_ref[...] = x_ref[...] + _dot(o_ref[...].astype(BF16), w_ref[...])


def _outproj_sample(o_pad, x, w_out):
    return pl.pallas_call(
        _outproj_sample_kernel,
        grid=(1,),
        in_specs=[
            pl.BlockSpec((TM, D_MODEL), lambda i: (0, 0)),
            pl.BlockSpec((TM, D_MODEL), lambda i: (N_PT, 0)),
            _const((D_MODEL, D_MODEL)),
        ],
        out_specs=pl.BlockSpec((TM, D_MODEL), lambda i: (N_PT, 0)),
        out_shape=jax.ShapeDtypeStruct((N_TOT, D_MODEL), F32),
        input_output_aliases={1: 0},
        compiler_params=_cparams(1),
    )(o_pad, x, w_out)


def _final_norm_kernel(x_ref, nrm_ref, yp_ref, ys_ref):
    i = pl.program_id(0)
    y = _rms(x_ref[...], nrm_ref[...])

    @pl.when(i < N_PT)
    def _():
        yp_ref[...] = y

    @pl.when(i == N_PT)
    def _():
        ys_ref[...] = y[:DEC_BATCH, :]


def _final_norm(x, nrm):
    return pl.pallas_call(
        _final_norm_kernel,
        grid=(N_TILES,),
        in_specs=[pl.BlockSpec((TM, D_MODEL), lambda i: (i, 0)), _const((1, D_MODEL))],
        out_specs=[
            pl.BlockSpec((TM, D_MODEL), lambda i: (jnp.minimum(i, N_PT - 1), 0)),
            pl.BlockSpec((DEC_BATCH, D_MODEL), lambda i: (0, 0)),
        ],
        out_shape=[
            jax.ShapeDtypeStruct((N_P, D_MODEL), F32),
            jax.ShapeDtypeStruct((DEC_BATCH, D_MODEL), F32),
        ],
        compiler_params=_cparams(1),
    )(x, nrm)


def _row(v):
    return v.reshape(1, -1).astype(F32)


def _prep_mix(w_s, b_s):
    causal = jnp.tril(jnp.ones((CHUNK, CHUNK), F32))
    eye = jnp.eye(CHUNK, dtype=F32)
    w0 = w_s * causal
    w1 = w_s[:, 0:1, 0:1] * eye
    b0 = jnp.repeat(b_s.T, D_A // A_GROUPS, axis=1)
    b1 = jnp.broadcast_to(jnp.repeat(b_s[:, 0], D_A // A_GROUPS)[None, :], (CHUNK, D_A))
    return jnp.stack([w0, w1]).astype(BF16), jnp.stack([b0, b1]).astype(F32)


def _prep_compress(cmp_pe, cmp_w1, cmp_w2):
    eye = jnp.eye(N_KV_GROUPS, dtype=F32)
    w1 = cmp_w1.reshape(2, 2, CMP_STRIDE, HEAD_DIM, HEAD_DIM)
    ws = jnp.einsum('khrde,gj->krgdhje', w1, eye).reshape(2, CMP_STRIDE * 256, 512).astype(BF16)
    cvec = jnp.einsum('kld,klde->ke', cmp_pe, cmp_w1, precision=HIGHEST)
    cvec = jnp.tile(cvec, (1, N_KV_GROUPS)).reshape(2, 1, 256).astype(F32)
    w2bd = jnp.einsum('ked,gj->kgejd', cmp_w2, eye).reshape(2, 256, 256).astype(BF16)
    return ws, cvec, w2bd


def _overlap():
    n = np.arange(N_CMP_PAD)[:, None] * CMP_STRIDE
    s0 = np.arange(LANES)[None, :] * SLC_BLOCK
    ovl = ((n < s0 + SLC_BLOCK) & (n + CMP_LEN > s0)).astype(np.float32)
    ovl[N_CMP_PAD - 1, :] = 0.0
    return ovl


def kernel(x_prompt, x_sample, cache_kv, state_win_kv, page_table, norm_mix, norm_ffn, norm_final, a_w_in, a_ln_g, a_ln_b, a_w_s, a_b_s, a_w_out, kv_norm, w_kv, cmp_pe, cmp_w1, cmp_w2, b_w_in, b_b_gate, b_w_out, f_w_gate, f_w_up, f_w_down, m_w_router, m_b_router, m_w_gate, m_w_up, m_w_down):
    x = jnp.concatenate([x_prompt.reshape(N_P, D_MODEL), x_sample.reshape(DEC_BATCH, D_MODEL),
                         jnp.zeros((TM - DEC_BATCH, D_MODEL), F32)], axis=0)
    cache3 = cache_kv.reshape(cache_kv.shape[0], PAGE_SIZE, 4 * N_KV_GROUPS * HEAD_DIM)
    state_win = state_win_kv.reshape(DEC_BATCH, WINDOW, 2 * N_KV_GROUPS * HEAD_DIM)
    ovl = _overlap()
    ovl_s = jnp.asarray(ovl)
    ovl_pt = jnp.asarray(ovl[:, :N_SEL_P].T.copy())
    expand = jnp.asarray((np.arange(PAST_LEN)[None, :] // SLC_BLOCK == np.arange(LANES)[:, None])
                         .astype(np.float32)).astype(BF16)
    slopes = jnp.asarray(np.repeat(np.asarray(_SLOPES, np.float32)[:, None], LANES, axis=1))

    v_p, v_s = [], []
    y_p = y_s = rows_p = rows_s = win_all = None
    for layer in range(DEPTH):
        if layer == N_A_LAYERS:
            wkv = w_kv.astype(BF16)
            wvt = jnp.concatenate([w_kv[:, 768:1024], w_kv[:, 1280:1536]], axis=1).T.astype(BF16)
            rows_p, rows_s, win_all, ksel, kwin, vselt, vwint = _kvproj(x, _row(kv_norm), wkv, wvt)
            ws, cvec, w2bd = _prep_compress(cmp_pe, cmp_w1, cmp_w2)
            kc_p, vct_p = _compress_prompt(rows_p, ws, cvec, w2bd)
            cache_seg = cache_kv.reshape(cache_kv.shape[0], PAGE_SIZE // CMP_STRIDE, CMP_STRIDE * KV_ROW)
            kc_s, vct_s = _compress_sample(page_table, cache_seg, ws, cvec, w2bd)
        if layer < N_A_LAYERS:
            wmix, bias = _prep_mix(a_w_s[layer], a_b_s[layer])
            x, vp, vs = _a_mixer(x, _row(norm_mix[layer]), a_w_in[layer].astype(BF16), _row(a_ln_g[layer]),
                                 _row(a_ln_b[layer]), wmix, bias, a_w_out[layer].astype(BF16))
            v_p.append(vp)
            v_s.append(vs)
        else:
            i = layer - N_A_LAYERS
            w_in = jnp.pad(b_w_in[i], ((0, 0), (0, LANES - 3 * N_HEADS))).astype(BF16)
            bg = jnp.pad(b_b_gate[i], (0, LANES - 3 * N_HEADS)).reshape(1, LANES).astype(F32)
            w_out = b_w_out[i].astype(BF16)
            q, gates = _qproj(x, _row(norm_mix[layer]), w_in, bg)
            q_s = q[N_P:N_P + DEC_BATCH].reshape(DEC_BATCH, 1, D_MODEL)
            g_s = gates[N_P:N_P + DEC_BATCH].reshape(DEC_BATCH, 1, LANES)
            kvn = rows_s[:DEC_BATCH, 512:].reshape(DEC_BATCH, 1, 512)
            winn = win_all[N_P:N_P + DEC_BATCH].reshape(DEC_BATCH, 1, 512)
            o_s = _nsa_sample(page_table, q_s, g_s, kc_s, vct_s, kvn, winn, state_win, cache3,
                              slopes, ovl_s, expand)
            x = _nsa_prompt(q, gates, x, kc_p, vct_p, ksel, vselt, kwin, vwint, ovl_pt, w_out)
            o_pad = jnp.pad(o_s.reshape(DEC_BATCH, D_MODEL), ((0, TM - DEC_BATCH), (0, 0)))
            x = _outproj_sample(o_pad, x, w_out)
        j = layer // 2
        if layer % 2 == 0:
            x = _ffn(x, _row(norm_ffn[layer]), f_w_gate[j].astype(BF16), f_w_up[j].astype(BF16),
                     f_w_down[j].astype(BF16))
        else:
            wr = jnp.pad(m_w_router[j], ((0, 0), (0, LANES - N_EXPERTS))).astype(F32)
            br = jnp.pad(m_b_router[j], (0, LANES - N_EXPERTS), constant_values=NEG_INF).reshape(1, LANES)
            x = _moe_layer(x, _row(norm_ffn[layer]), wr, br.astype(F32), m_w_gate[j].astype(BF16),
                           m_w_up[j].astype(BF16), m_w_down[j].astype(BF16))
    y_p, y_s = _final_norm(x, _row(norm_final))

    kvshape = (4, N_KV_GROUPS, HEAD_DIM)
    winshape = (2, N_KV_GROUPS, HEAD_DIM)
    win_p = win_all[:N_P].reshape(BATCH, SEQ, 512)[:, SEQ - WINDOW:]
    win_s = jnp.concatenate([state_win[:, 1:], win_all[N_P:N_P + DEC_BATCH].reshape(DEC_BATCH, 1, 512)], axis=1)
    return (y_p.reshape(BATCH, SEQ, D_MODEL),
            y_s.reshape(DEC_BATCH, 1, D_MODEL),
            rows_p.reshape((BATCH, SEQ) + kvshape),
            rows_s[:DEC_BATCH].reshape((DEC_BATCH, 1) + kvshape),
            win_p.reshape((BATCH, WINDOW) + winshape),
            win_s.reshape((DEC_BATCH, WINDOW) + winshape),
            jnp.stack(v_p),
            jnp.stack(v_s).reshape(N_A_LAYERS, DEC_BATCH, 1, D_A))
```

```python
import functools

import numpy as np
import jax
import jax.numpy as jnp
from jax import lax
from jax.experimental import pallas as pl
from jax.experimental.pallas import tpu as pltpu

F32 = jnp.float32
BF16 = jnp.bfloat16
HIGHEST = lax.Precision.HIGHEST

D_MODEL = 1024
BATCH = 8
SEQ = 2048
DEPTH = 4
DEC_BATCH = 128
PAST_LEN = 2048
PAGE_SIZE = 128
N_A_LAYERS = DEPTH // 2
CHUNK = 128
D_A = D_MODEL
A_GROUPS = 8
N_HEADS = 16
HEAD_DIM = 64
N_KV_GROUPS = 4
Q_PER_GROUP = 4
CMP_LEN = 32
CMP_STRIDE = 16
SLC_BLOCK = 64
N_TOP = 16
WINDOW = 512
D_FF = 2816
N_EXPERTS = 8
TOP_K = 2
RMS_EPS = 1e-6
LN_EPS = 1e-5
NEG_INF = -1e30
FORCE_SCORE = 1e4

LANES = 128
TM = 512
N_P = BATCH * SEQ
N_PT = N_P // TM
N_TOT = N_P + TM
N_TILES = N_TOT // TM
N_REAL = N_P + DEC_BATCH
TQ = 128
BK_SEL = 256
BK_WIN = 128
N_WIN_KEYS = WINDOW + TQ
N_CMP_PAD = 128
N_SEL_P = SEQ // SLC_BLOCK
N_PAGES = PAST_LEN // PAGE_SIZE
GD = N_KV_GROUPS * HEAD_DIM
KV_ROW = 4 * GD
KV_WIN = 2 * GD
T_MOE = 512
N_ASSIGN = N_REAL * TOP_K
N_MOE_BLOCKS = -(-N_ASSIGN // T_MOE) + N_EXPERTS
N_MOE_ROWS = N_MOE_BLOCKS * T_MOE
FF_SPLIT = 2
VMEM_LIMIT = 56 * 1024 * 1024

_SLOPES = [float(v) for v in
           (2.0 ** (-8.0 * np.arange(1, N_HEADS + 1, dtype=np.float32) / N_HEADS)).astype(np.float32)]


def _cparams(n_axes):
    return pltpu.CompilerParams(dimension_semantics=("arbitrary",) * n_axes,
                                vmem_limit_bytes=VMEM_LIMIT)


def _const(shape):
    nd = len(shape)
    return pl.BlockSpec(shape, lambda *_: (0,) * nd, pipeline_mode=pl.Buffered(1))


def _rms(x, g):
    return x * lax.rsqrt(jnp.mean(x * x, axis=-1, keepdims=True) + RMS_EPS) * g


def _dot(a, b, **kw):
    return jnp.dot(a, b, preferred_element_type=F32, **kw)


def _dot_nt(a, b, **kw):
    return lax.dot_general(a, b, (((1,), (1,)), ((), ())), preferred_element_type=F32, **kw)


def _a_mixer_kernel(x_ref, nrm_ref, win_ref, lng_ref, lnb_ref, wmix_ref, bias_ref, wout_ref,
                    xo_ref, vp_ref, vs_ref, mixed_ref):
    i = pl.program_id(0)
    x = x_ref[...]
    h = _rms(x, nrm_ref[...]).astype(BF16)
    z = jax.nn.gelu(_dot(h, win_ref[...]))
    u = z[:, :D_A]
    v = z[:, D_A:]
    mu = jnp.mean(v, axis=-1, keepdims=True)
    var = jnp.mean(jnp.square(v - mu), axis=-1, keepdims=True)
    v = (v - mu) * lax.rsqrt(var + LN_EPS) * lng_ref[...] + lnb_ref[...]

    @pl.when((i < N_PT) & (i % (SEQ // TM) == SEQ // TM - 1))
    def _():
        vp_ref[0] = v[TM - CHUNK:, :]

    @pl.when(i == N_PT)
    def _():
        vs_ref[...] = v[:DEC_BATCH, :]

    vb = v.astype(BF16)
    for c in range(TM // CHUNK):
        for g in range(A_GROUPS):
            cols = slice(g * LANES, (g + 1) * LANES)
            rows = slice(c * CHUNK, (c + 1) * CHUNK)
            mixed_ref[rows, cols] = _dot(wmix_ref[0, g], vb[rows, cols]) + bias_ref[0, :, cols]
    t = (u * mixed_ref[...]).astype(BF16)
    xo_ref[...] = x + _dot(t, wout_ref[...])


def _a_mixer(x, nrm, w_in, ln_g, ln_b, wmix, bias, w_out):
    return pl.pallas_call(
        _a_mixer_kernel,
        grid=(N_TILES,),
        in_specs=[
            pl.BlockSpec((TM, D_MODEL), lambda i: (i, 0)),
            _const((1, D_MODEL)),
            _const((D_MODEL, 2 * D_A)),
            _const((1, D_A)),
            _const((1, D_A)),
            pl.BlockSpec((1, A_GROUPS, CHUNK, CHUNK), lambda i: (i // N_PT, 0, 0, 0)),
            pl.BlockSpec((1, CHUNK, D_A), lambda i: (i // N_PT, 0, 0)),
            _const((D_A, D_MODEL)),
        ],
        out_specs=[
            pl.BlockSpec((TM, D_MODEL), lambda i: (i, 0)),
            pl.BlockSpec((1, CHUNK, D_A), lambda i: (jnp.minimum(i // (SEQ // TM), BATCH - 1), 0, 0)),
            pl.BlockSpec((DEC_BATCH, D_A), lambda i: (0, 0)),
        ],
        out_shape=[
            jax.ShapeDtypeStruct((N_TOT, D_MODEL), F32),
            jax.ShapeDtypeStruct((BATCH, CHUNK, D_A), F32),
            jax.ShapeDtypeStruct((DEC_BATCH, D_A), F32),
        ],
        scratch_shapes=[pltpu.VMEM((TM, D_A), F32)],
        compiler_params=_cparams(1),
    )(x, nrm, w_in, ln_g, ln_b, wmix, bias, w_out)


def _swiglu_block(h, wg_ref, wu_ref, wd_ref, lead):
    ffh = D_FF // FF_SPLIT
    out = None
    for s in range(FF_SPLIT):
        cols = slice(s * ffh, (s + 1) * ffh)
        g = _dot(h, wg_ref[lead + (slice(None), cols)])
        u = _dot(h, wu_ref[lead + (slice(None), cols)])
        a = (jax.nn.silu(g) * u).astype(BF16)
        part = _dot(a, wd_ref[lead + (cols, slice(None))])
        out = part if out is None else out + part
    return out


def _ffn_kernel(x_ref, nrm_ref, wg_ref, wu_ref, wd_ref, o_ref):
    x = x_ref[...]
    h = _rms(x, nrm_ref[...]).astype(BF16)
    o_ref[...] = x + _swiglu_block(h, wg_ref, wu_ref, wd_ref, ())


def _ffn(x, nrm, wg, wu, wd):
    return pl.pallas_call(
        _ffn_kernel,
        grid=(N_TILES,),
        in_specs=[
            pl.BlockSpec((TM, D_MODEL), lambda i: (i, 0)),
            _const((1, D_MODEL)),
            _const((D_MODEL, D_FF)),
            _const((D_MODEL, D_FF)),
            _const((D_FF, D_MODEL)),
        ],
        out_specs=pl.BlockSpec((TM, D_MODEL), lambda i: (i, 0)),
        out_shape=jax.ShapeDtypeStruct((N_TOT, D_MODEL), F32),
        compiler_params=_cparams(1),
    )(x, nrm, wg, wu, wd)


def _router_kernel(x_ref, nrm_ref, wr_ref, br_ref, h_ref, r_ref):
    h = _rms(x_ref[...], nrm_ref[...])
    h_ref[...] = h.astype(BF16)
    logits = _dot(h, wr_ref[...], precision=HIGHEST) + br_ref[...]
    lane = lax.broadcasted_iota(jnp.int32, logits.shape, 1).astype(F32)
    big = float(LANES)
    m1 = jnp.max(logits, axis=1, keepdims=True)
    i1 = jnp.min(jnp.where(logits == m1, lane, big), axis=1, keepdims=True)
    l2 = jnp.where(lane == i1, -jnp.inf, logits)
    m2 = jnp.max(l2, axis=1, keepdims=True)
    i2 = jnp.min(jnp.where(l2 == m2, lane, big), axis=1, keepdims=True)
    e = jnp.exp(m2 - m1)
    g1 = 1.0 / (1.0 + e)
    g2 = e / (1.0 + e)
    r_ref[...] = jnp.where(lane == 0.0, i1, jnp.where(lane == 1.0, i2,
                           jnp.where(lane == 2.0, g1, jnp.where(lane == 3.0, g2, 0.0))))


def _router(x, nrm, wr, br):
    return pl.pallas_call(
        _router_kernel,
        grid=(N_TILES,),
        in_specs=[
            pl.BlockSpec((TM, D_MODEL), lambda i: (i, 0)),
            _const((1, D_MODEL)),
            _const((D_MODEL, LANES)),
            _const((1, LANES)),
        ],
        out_specs=[
            pl.BlockSpec((TM, D_MODEL), lambda i: (i, 0)),
            pl.BlockSpec((TM, LANES), lambda i: (i, 0)),
        ],
        out_shape=[
            jax.ShapeDtypeStruct((N_TOT, D_MODEL), BF16),
            jax.ShapeDtypeStruct((N_TOT, LANES), F32),
        ],
        compiler_params=_cparams(1),
    )(x, nrm, wr, br)


def _moe_kernel(be_ref, na_ref, x_ref, wg_ref, wu_ref, wd_ref, o_ref):
    i = pl.program_id(0)

    @pl.when(i < na_ref[0])
    def _():
        o_ref[...] = _swiglu_block(x_ref[...], wg_ref, wu_ref, wd_ref, (0, 0))

    @pl.when(i >= na_ref[0])
    def _():
        o_ref[...] = jnp.zeros(o_ref.shape, o_ref.dtype)


def _moe_experts(blk_expert, n_active, x_rows, wg, wu, wd, layer):
    return pl.pallas_call(
        _moe_kernel,
        grid_spec=pltpu.PrefetchScalarGridSpec(
            num_scalar_prefetch=2,
            grid=(N_MOE_BLOCKS,),
            in_specs=[
                pl.BlockSpec((T_MOE, D_MODEL), lambda i, be, na: (i, 0)),
                pl.BlockSpec((1, 1, D_MODEL, D_FF), lambda i, be, na: (layer, be[i], 0, 0)),
                pl.BlockSpec((1, 1, D_MODEL, D_FF), lambda i, be, na: (layer, be[i], 0, 0)),
                pl.BlockSpec((1, 1, D_FF, D_MODEL), lambda i, be, na: (layer, be[i], 0, 0)),
            ],
            out_specs=pl.BlockSpec((T_MOE, D_MODEL), lambda i, be, na: (i, 0)),
        ),
        out_shape=jax.ShapeDtypeStruct((N_MOE_ROWS, D_MODEL), F32),
        compiler_params=_cparams(1),
    )(blk_expert, n_active, x_rows, wg, wu, wd)


def _moe_layer(x, nrm, wr, br, wg, wu, wd, layer):
    h, r = _router(x, nrm, wr, br)
    r = r[:N_REAL]
    expert = r[:, 0:2].astype(jnp.int32).reshape(N_ASSIGN)
    gate = r[:, 2:4]
    onehot = (expert[:, None] == jnp.arange(N_EXPERTS, dtype=jnp.int32)[None, :]).astype(jnp.int32)
    csum = jnp.cumsum(onehot, axis=0)
    counts = csum[-1]
    rank = jnp.sum(csum * onehot, axis=1) - 1
    padded = (counts + T_MOE - 1) // T_MOE * T_MOE
    pad_end = jnp.cumsum(padded)
    pad_start = pad_end - padded
    dest = jnp.sum(pad_start[None, :] * onehot, axis=1) + rank
    blk_start = jnp.arange(N_MOE_BLOCKS, dtype=jnp.int32) * T_MOE
    blk_expert = jnp.minimum(jnp.sum(pad_end[None, :] <= blk_start[:, None], axis=1),
                             N_EXPERTS - 1).astype(jnp.int32)
    n_active = (pad_end[-1:] // T_MOE).astype(jnp.int32)
    x_rows = jnp.zeros((N_MOE_ROWS, D_MODEL), BF16).at[dest].set(jnp.repeat(h[:N_REAL], TOP_K, axis=0))
    y_rows = _moe_experts(blk_expert, n_active, x_rows, wg, wu, wd, layer)
    d2 = dest.reshape(N_REAL, TOP_K)
    y = gate[:, 0:1] * y_rows[d2[:, 0]] + gate[:, 1:2] * y_rows[d2[:, 1]]
    return x.at[:N_REAL].add(y)


def _kvproj_kernel(x_ref, nrm_ref, wkv_ref, wkvt_ref, rowst_ref, wint_ref, kvt_s_ref, kvn_s_ref, cmp_ref,
                   ksel_ref, kwin_ref, vselt_ref, vwint_ref):
    i = pl.program_id(0)
    h = _rms(x_ref[...], nrm_ref[...]).astype(BF16)
    kv = _dot(h, wkv_ref[...])
    kvt = _dot_nt(wkvt_ref[...], h)

    @pl.when(i < N_PT)
    def _():
        rowst_ref[0] = kvt[:KV_ROW, :]
        wint_ref[0] = kvt[KV_ROW:, :]
        for cb in range(2 * GD // LANES):
            cmp_ref[cb] = kv[:, cb * LANES:(cb + 1) * LANES]

    @pl.when(i == N_PT)
    def _():
        kvt_s_ref[...] = kvt[:, :DEC_BATCH]
        kvn_s_ref[...] = kv[:DEC_BATCH, :]

    kvt_b = kvt.astype(BF16)
    for g in range(N_KV_GROUPS):
        ksel_ref[g] = kv[:, 2 * GD + g * HEAD_DIM:2 * GD + (g + 1) * HEAD_DIM].astype(BF16)
        kwin_ref[g] = kv[:, 4 * GD + g * HEAD_DIM:4 * GD + (g + 1) * HEAD_DIM].astype(BF16)
        for j in range(TM // BK_SEL):
            vselt_ref[g, j] = kvt_b[3 * GD + g * HEAD_DIM:3 * GD + (g + 1) * HEAD_DIM,
                                    j * BK_SEL:(j + 1) * BK_SEL]
        for j in range(TM // BK_WIN):
            vwint_ref[g, j] = kvt_b[5 * GD + g * HEAD_DIM:5 * GD + (g + 1) * HEAD_DIM,
                                    j * BK_WIN:(j + 1) * BK_WIN]


def _kvproj(x, nrm, wkv, wkvt):
    n_kv = KV_ROW + KV_WIN
    tiles_per_seq = SEQ // TM
    ip = lambda i: jnp.minimum(i, N_PT - 1)
    return pl.pallas_call(
        _kvproj_kernel,
        grid=(N_TILES,),
        in_specs=[
            pl.BlockSpec((TM, D_MODEL), lambda i: (i, 0)),
            _const((1, D_MODEL)),
            _const((D_MODEL, n_kv)),
            _const((n_kv, D_MODEL)),
        ],
        out_specs=[
            pl.BlockSpec((1, KV_ROW, TM), lambda i: (ip(i) // tiles_per_seq, 0, ip(i) % tiles_per_seq)),
            pl.BlockSpec((1, KV_WIN, TM), lambda i: (ip(i) // tiles_per_seq, 0, ip(i) % tiles_per_seq)),
            pl.BlockSpec((n_kv, DEC_BATCH), lambda i: (0, 0)),
            pl.BlockSpec((DEC_BATCH, n_kv), lambda i: (0, 0)),
            pl.BlockSpec((2 * GD // LANES, TM, LANES), lambda i: (0, ip(i), 0)),
            pl.BlockSpec((N_KV_GROUPS, TM, HEAD_DIM), lambda i: (0, i, 0)),
            pl.BlockSpec((N_KV_GROUPS, TM, HEAD_DIM), lambda i: (0, i, 0)),
            pl.BlockSpec((N_KV_GROUPS, TM // BK_SEL, HEAD_DIM, BK_SEL), lambda i: (0, i, 0, 0)),
            pl.BlockSpec((N_KV_GROUPS, TM // BK_WIN, HEAD_DIM, BK_WIN), lambda i: (0, i, 0, 0)),
        ],
        out_shape=[
            jax.ShapeDtypeStruct((BATCH, KV_ROW, SEQ), F32),
            jax.ShapeDtypeStruct((BATCH, KV_WIN, SEQ), F32),
            jax.ShapeDtypeStruct((n_kv, DEC_BATCH), F32),
            jax.ShapeDtypeStruct((DEC_BATCH, n_kv), F32),
            jax.ShapeDtypeStruct((2 * GD // LANES, N_P, LANES), F32),
            jax.ShapeDtypeStruct((N_KV_GROUPS, N_TOT, HEAD_DIM), BF16),
            jax.ShapeDtypeStruct((N_KV_GROUPS, N_TOT, HEAD_DIM), BF16),
            jax.ShapeDtypeStruct((N_KV_GROUPS, N_TOT // BK_SEL, HEAD_DIM, BK_SEL), BF16),
            jax.ShapeDtypeStruct((N_KV_GROUPS, N_TOT // BK_WIN, HEAD_DIM, BK_WIN), BF16),
        ],
        compiler_params=_cparams(1),
    )(x, nrm, wkv, wkvt)


def _compress_body(nat_ref, lead, ws_ref, c_ref, w2_ref, kc_ref, vct_ref):
    n_seg = SEQ // CMP_STRIDE
    for kind in range(2):
        y = None
        for r in range(CMP_STRIDE):
            xr = jnp.concatenate(
                [nat_ref[lead + (kind * (GD // LANES) + j, pl.ds(r, n_seg, stride=CMP_STRIDE), slice(None))]
                 for j in range(GD // LANES)], axis=1)
            part = _dot(xr.astype(BF16), ws_ref[kind, r * GD:(r + 1) * GD, :])
            y = part if y is None else y + part
        a = y[:, :GD]
        b = pltpu.roll(y[:, GD:], N_CMP_PAD - 1, axis=0)
        hid = jax.nn.gelu(a + b + c_ref[kind])
        out = _dot(hid.astype(BF16), w2_ref[kind])
        if kind == 0:
            kc_ref[0] = out.astype(BF16)
        else:
            vct_ref[0] = out.T.astype(BF16)


def _compress_prompt_kernel(cmp_ref, ws_ref, c_ref, w2_ref, kc_ref, vct_ref):
    _compress_body(cmp_ref, (), ws_ref, c_ref, w2_ref, kc_ref, vct_ref)


def _compress_sample_kernel(pt_ref, *refs):
    pages = refs[:N_PAGES]
    ws_ref, c_ref, w2_ref, kc_ref, vct_ref, nat_ref = refs[N_PAGES:]
    for p, page in enumerate(pages):
        for cb in range(2 * GD // LANES):
            nat_ref[cb, p * PAGE_SIZE:(p + 1) * PAGE_SIZE, :] = page[0, cb * LANES:(cb + 1) * LANES, :].T
    _compress_body(nat_ref, (), ws_ref, c_ref, w2_ref, kc_ref, vct_ref)


_CMP_W_SHAPES = [(2, CMP_STRIDE * GD, 2 * GD), (2, 1, GD), (2, GD, GD)]


def _compress_prompt(cmp_nat, ws, cvec, w2bd):
    return pl.pallas_call(
        _compress_prompt_kernel,
        grid=(BATCH,),
        in_specs=[pl.BlockSpec((2 * GD // LANES, SEQ, LANES), lambda b: (0, b, 0))]
        + [_const(s) for s in _CMP_W_SHAPES],
        out_specs=[
            pl.BlockSpec((1, N_CMP_PAD, GD), lambda b: (b, 0, 0)),
            pl.BlockSpec((1, GD, N_CMP_PAD), lambda b: (b, 0, 0)),
        ],
        out_shape=[
            jax.ShapeDtypeStruct((BATCH, N_CMP_PAD, GD), BF16),
            jax.ShapeDtypeStruct((BATCH, GD, N_CMP_PAD), BF16),
        ],
        compiler_params=_cparams(1),
    )(cmp_nat, ws, cvec, w2bd)


def _page_spec(p, half):
    return pl.BlockSpec((1, 2 * GD, PAGE_SIZE), lambda b, pt: (pt[b, p], half, 0))


def _const_sp(shape):
    return pl.BlockSpec(shape, functools.partial(lambda nd, b, pt: (0,) * nd, len(shape)),
                        pipeline_mode=pl.Buffered(1))


def _compress_sample(page_table, cache_t, ws, cvec, w2bd):
    return pl.pallas_call(
        _compress_sample_kernel,
        grid_spec=pltpu.PrefetchScalarGridSpec(
            num_scalar_prefetch=1,
            grid=(DEC_BATCH,),
            in_specs=[_page_spec(p, 0) for p in range(N_PAGES)] + [_const_sp(s) for s in _CMP_W_SHAPES],
            out_specs=[
                pl.BlockSpec((1, N_CMP_PAD, GD), lambda b, pt: (b, 0, 0)),
                pl.BlockSpec((1, GD, N_CMP_PAD), lambda b, pt: (b, 0, 0)),
            ],
            scratch_shapes=[pltpu.VMEM((2 * GD // LANES, PAST_LEN, LANES), F32)],
        ),
        out_shape=[
            jax.ShapeDtypeStruct((DEC_BATCH, N_CMP_PAD, GD), BF16),
            jax.ShapeDtypeStruct((DEC_BATCH, GD, N_CMP_PAD), BF16),
        ],
        compiler_params=_cparams(1),
    )(page_table, *([cache_t] * N_PAGES), ws, cvec, w2bd)


def _qproj_kernel(x_ref, nrm_ref, w_ref, bg_ref, q_ref, g_ref):
    h = _rms(x_ref[...], nrm_ref[...]).astype(BF16)
    p = _dot(h, w_ref[...])
    q_ref[...] = p[:, :D_MODEL] * (HEAD_DIM ** -0.5)
    g_ref[...] = jax.nn.sigmoid(p[:, D_MODEL:] + bg_ref[...])


def _qproj(x, nrm, w, bg):
    return pl.pallas_call(
        _qproj_kernel,
        grid=(N_TILES,),
        in_specs=[
            pl.BlockSpec((TM, D_MODEL), lambda i: (i, 0)),
            _const((1, D_MODEL)),
            _const((D_MODEL, D_MODEL + LANES)),
            _const((1, LANES)),
        ],
        out_specs=[
            pl.BlockSpec((TM, D_MODEL), lambda i: (i, 0)),
            pl.BlockSpec((TM, LANES), lambda i: (i, 0)),
        ],
        out_shape=[
            jax.ShapeDtypeStruct((N_TOT, D_MODEL), F32),
            jax.ShapeDtypeStruct((N_TOT, LANES), F32),
        ],
        compiler_params=_cparams(1),
    )(x, nrm, w, bg)


def _top_mask(score, n_sel, idx, axis):
    cnt = jnp.zeros(score.shape, F32)
    for i in range(n_sel):
        row = score[i:i + 1, :] if axis == 0 else score[:, i:i + 1]
        beats = (row > score) | ((row == score) & (idx > i))
        cnt = cnt + jnp.where(beats, 1.0, 0.0)
    return jnp.where(cnt < float(N_TOP), 0.0, NEG_INF)


def _nsa_prompt_kernel(q_ref, gt_ref, x_ref, kc_ref, vct_ref, ks_ref, vst_ref, kw_ref, vwt_ref,
                       ovl_ref, wout_ref, o_ref, qgt_ref, ot_ref, selneg_ref, m_ref, l_ref, acc_ref):
    qi = pl.program_id(1)
    t0 = qi * TQ
    qt_all = q_ref[...].T
    gt = gt_ref[...].T
    tq = t0 + lax.broadcasted_iota(jnp.int32, (1, TQ), 1)
    groups = range(N_KV_GROUPS)
    heads = range(Q_PER_GROUP)
    lanes = [slice(r * TQ, (r + 1) * TQ) for r in heads]

    n_idx = lax.broadcasted_iota(jnp.int32, (N_CMP_PAD, 1), 0)
    d_c = tq - (n_idx * CMP_STRIDE + CMP_LEN - 1)
    ok_c = d_c >= 0
    d_cf = d_c.astype(F32)
    j_idx = lax.broadcasted_iota(jnp.int32, (N_SEL_P, 1), 0)
    cur = tq // SLC_BLOCK
    valid = j_idx <= cur
    forced = (j_idx == 0) | (j_idx == cur) | (j_idx == cur - 1)

    oc_t = []
    for g in groups:
        qgt_ref[g] = jnp.concatenate(
            [qt_all[(g * Q_PER_GROUP + r) * HEAD_DIM:(g * Q_PER_GROUP + r + 1) * HEAD_DIM, :]
             for r in heads], axis=1).astype(BF16)
    sts_c = [_dot(kc_ref[0, :, g * HEAD_DIM:(g + 1) * HEAD_DIM], qgt_ref[g]) for g in groups]
    for g in groups:
        st = sts_c[g]
        psum = jnp.zeros((N_CMP_PAD, TQ), F32)
        ps = []
        for r in heads:
            s = st[:, lanes[r]] - _SLOPES[g * Q_PER_GROUP + r] * d_cf
            s = jnp.where(ok_c, s, NEG_INF)
            p = jnp.exp(s - jnp.max(s, axis=0, keepdims=True))
            p = p / jnp.sum(p, axis=0, keepdims=True)
            p = jnp.where(ok_c, p, 0.0)
            psum = psum + p
            ps.append(p.astype(BF16))
        oc_t.append(_dot(vct_ref[0, g * HEAD_DIM:(g + 1) * HEAD_DIM, :], jnp.concatenate(ps, axis=1)))
        imp = _dot(ovl_ref[...], psum, precision=HIGHEST)
        score = jnp.where(valid, jnp.where(forced, FORCE_SCORE, imp), NEG_INF)
        selneg_ref[g] = _top_mask(score, N_SEL_P, j_idx, 0)

    m_ref[...] = jnp.full(m_ref.shape, NEG_INF, F32)
    l_ref[...] = jnp.zeros(l_ref.shape, F32)
    acc_ref[...] = jnp.zeros(acc_ref.shape, F32)
    d0 = (lax.broadcasted_iota(jnp.int32, (BK_SEL, TQ), 1)
          - lax.broadcasted_iota(jnp.int32, (BK_SEL, TQ), 0))
    blocks_per_chunk = BK_SEL // SLC_BLOCK

    def sel_body(c, carry):
        d = d0 + (t0 - c * BK_SEL)
        causal = d >= 0
        df = d.astype(F32)
        k_rows = pl.ds(pl.multiple_of(c * BK_SEL, BK_SEL), BK_SEL)
        sts = [_dot(ks_ref[g, k_rows, :], qgt_ref[g]) for g in groups]
        m_all = m_ref[...]
        l_all = l_ref[...]
        pvs, alpha_all, m_out, l_out = [], [], [], []
        for g in groups:
            rows = [jnp.broadcast_to(selneg_ref[g, pl.ds(c * blocks_per_chunk + jj, 1), :], (SLC_BLOCK, TQ))
                    for jj in range(blocks_per_chunk)]
            madd = jnp.where(causal, jnp.concatenate(rows, axis=0), NEG_INF)
            ps, alphas, ms, ls = [], [], [], []
            for r in heads:
                s = sts[g][:, lanes[r]] - _SLOPES[g * Q_PER_GROUP + r] * df + madd
                m_old = m_all[g, :, lanes[r]]
                m_new = jnp.maximum(m_old, jnp.max(s, axis=0, keepdims=True))
                alpha = jnp.exp(m_old - m_new)
                p = jnp.exp(s - m_new)
                ls.append(alpha * l_all[g, :, lanes[r]] + jnp.sum(p, axis=0, keepdims=True))
                ms.append(m_new)
                ps.append(p.astype(BF16))
                alphas.append(alpha)
            pvs.append(_dot(vst_ref[g, c], jnp.concatenate(ps, axis=1)))
            alpha_all.append(jnp.concatenate(alphas, axis=1))
            m_out.append(jnp.concatenate(ms, axis=1))
            l_out.append(jnp.concatenate(ls, axis=1))
        for g in groups:
            m_ref[g] = m_out[g]
            l_ref[g] = l_out[g]
            acc_ref[g] = acc_ref[g] * alpha_all[g] + pvs[g]
        return carry

    lax.fori_loop(0, (t0 + TQ - 1) // BK_SEL + 1, sel_body, 0)

    k_start = pl.multiple_of(jnp.maximum(t0 - WINDOW, 0), BK_WIN)
    c_start = k_start // BK_WIN
    d_w = (lax.broadcasted_iota(jnp.int32, (N_WIN_KEYS, TQ), 1)
           - lax.broadcasted_iota(jnp.int32, (N_WIN_KEYS, TQ), 0)) + (t0 - k_start)
    madd_w = jnp.where((d_w >= 0) & (d_w < WINDOW), 0.0, NEG_INF)
    d_wf = d_w.astype(F32)

    sts_w = [_dot(kw_ref[g, pl.ds(k_start, N_WIN_KEYS), :], qgt_ref[g]) for g in groups]
    for g in groups:
        st = sts_w[g]
        ps = []
        ls = []
        for r in heads:
            s = st[:, lanes[r]] - _SLOPES[g * Q_PER_GROUP + r] * d_wf + madd_w
            p = jnp.exp(s - jnp.max(s, axis=0, keepdims=True))
            ls.append(jnp.sum(p, axis=0, keepdims=True))
            ps.append(p.astype(BF16))
        pt = jnp.concatenate(ps, axis=1)
        ow_t = None
        for c in range(N_WIN_KEYS // BK_WIN):
            part = _dot(vwt_ref[g, c_start + c], pt[c * BK_WIN:(c + 1) * BK_WIN, :])
            ow_t = part if ow_t is None else ow_t + part
        ow_t = ow_t / jnp.concatenate(ls, axis=1)
        os_t = acc_ref[g] / l_ref[g]
        for r in heads:
            hd = g * Q_PER_GROUP + r
            ot_ref[hd * HEAD_DIM:(hd + 1) * HEAD_DIM, :] = (
                gt[3 * hd:3 * hd + 1, :] * oc_t[g][:, lanes[r]]
                + gt[3 * hd + 1:3 * hd + 2, :] * os_t[:, lanes[r]]
                + gt[3 * hd + 2:3 * hd + 3, :] * ow_t[:, lanes[r]])

    o = ot_ref[...].T.astype(BF16)
    o_ref[...] = x_ref[...] + _dot(o, wout_ref[...])


def _nsa_prompt(q, gates, x, kc, vct, ksel, vselt, kwin, vwint, ovl_t, w_out):
    nq = SEQ // TQ
    tile = lambda b, qi: (b * nq + qi, 0)
    wide = Q_PER_GROUP * TQ
    return pl.pallas_call(
        _nsa_prompt_kernel,
        grid=(BATCH, nq),
        in_specs=[
            pl.BlockSpec((TQ, D_MODEL), tile),
            pl.BlockSpec((TQ, LANES), tile),
            pl.BlockSpec((TQ, D_MODEL), tile),
            pl.BlockSpec((1, N_CMP_PAD, GD), lambda b, qi: (b, 0, 0)),
            pl.BlockSpec((1, GD, N_CMP_PAD), lambda b, qi: (b, 0, 0)),
            pl.BlockSpec((N_KV_GROUPS, SEQ, HEAD_DIM), lambda b, qi: (0, b, 0)),
            pl.BlockSpec((N_KV_GROUPS, SEQ // BK_SEL, HEAD_DIM, BK_SEL), lambda b, qi: (0, b, 0, 0)),
            pl.BlockSpec((N_KV_GROUPS, SEQ, HEAD_DIM), lambda b, qi: (0, b, 0)),
            pl.BlockSpec((N_KV_GROUPS, SEQ // BK_WIN, HEAD_DIM, BK_WIN), lambda b, qi: (0, b, 0, 0)),
            _const((N_SEL_P, N_CMP_PAD)),
            _const((D_MODEL, D_MODEL)),
        ],
        out_specs=pl.BlockSpec((TQ, D_MODEL), tile),
        out_shape=jax.ShapeDtypeStruct((N_TOT, D_MODEL), F32),
        input_output_aliases={2: 0},
        scratch_shapes=[
            pltpu.VMEM((N_KV_GROUPS, HEAD_DIM, wide), BF16),
            pltpu.VMEM((D_MODEL, TQ), F32),
            pltpu.VMEM((N_KV_GROUPS, N_SEL_P, TQ), F32),
            pltpu.VMEM((N_KV_GROUPS, 1, wide), F32),
            pltpu.VMEM((N_KV_GROUPS, 1, wide), F32),
            pltpu.VMEM((N_KV_GROUPS, HEAD_DIM, wide), F32),
        ],
        compiler_params=_cparams(2),
    )(q, gates, x, kc, vct, ksel, vselt, kwin, vwint, ovl_t, w_out)


def _nsa_sample_kernel(pt_ref, q_ref, g_ref, kc_ref, vct_ref, kvn_ref, wcol_ref, win_ref, *refs):
    pages = refs[:N_PAGES]
    slope_ref, ovl_ref, exp_ref, o_ref, wino_ref = refs[N_PAGES:]
    b = pl.program_id(0)
    t = PAST_LEN
    q = q_ref[0]
    qh = jnp.concatenate([q[:, h * HEAD_DIM:(h + 1) * HEAD_DIM] for h in range(N_HEADS)], axis=0)
    q4 = jnp.concatenate([qh] * N_KV_GROUPS, axis=1)
    hrow = lax.broadcasted_iota(jnp.int32, (N_HEADS, GD), 0)
    col = lax.broadcasted_iota(jnp.int32, (N_HEADS, GD), 1)
    own = (col // HEAD_DIM) == (hrow // Q_PER_GROUP)
    qbd = jnp.where(own, q4, 0.0).astype(BF16)
    qbd_f = qbd.astype(F32)
    slope = slope_ref[:, 0:1]
    kvn = kvn_ref[0]

    def new_key_score(k_new):
        return jnp.sum(qbd_f * k_new.astype(BF16).astype(F32), axis=1, keepdims=True)

    def new_val(p_new, v_new):
        return p_new.astype(BF16).astype(F32) * v_new.astype(BF16).astype(F32)

    lane_c = lax.broadcasted_iota(jnp.int32, (1, N_CMP_PAD), 1)
    d_c = t - (lane_c * CMP_STRIDE + CMP_LEN - 1)
    ok_c = d_c >= 0
    s = _dot_nt(qbd, kc_ref[0]) - slope * d_c.astype(F32)
    s = jnp.where(ok_c, s, NEG_INF)
    p = jnp.exp(s - jnp.max(s, axis=1, keepdims=True))
    p = p / jnp.sum(p, axis=1, keepdims=True)
    p = jnp.where(ok_c, p, 0.0)
    o_c = _dot_nt(p.astype(BF16), vct_ref[0])

    h16r = lax.broadcasted_iota(jnp.int32, (N_HEADS, N_HEADS), 0) // Q_PER_GROUP
    h16c = lax.broadcasted_iota(jnp.int32, (N_HEADS, N_HEADS), 1) // Q_PER_GROUP
    pg = _dot(jnp.where(h16r == h16c, 1.0, 0.0), p, precision=HIGHEST)
    imp = _dot(pg, ovl_ref[...], precision=HIGHEST)
    j_idx = lax.broadcasted_iota(jnp.int32, (1, LANES), 1)
    cur = t // SLC_BLOCK
    forced = (j_idx == 0) | (j_idx == cur) | (j_idx == cur - 1)
    score = jnp.where(j_idx <= cur, jnp.where(forced, FORCE_SCORE, imp), NEG_INF)
    selneg = _top_mask(score, cur + 1, j_idx, 1)
    sel = jnp.where(selneg == 0.0, 1.0, 0.0).astype(BF16)
    selexp = _dot(sel, exp_ref[...])

    s_all = jnp.concatenate([_dot(qbd, pg_ref[0, 0:GD, :].astype(BF16)) for pg_ref in pages], axis=1)
    pos = lax.broadcasted_iota(jnp.int32, (1, PAST_LEN), 1)
    s_all = jnp.where(selexp > 0.5, s_all - slope * (t - pos).astype(F32), NEG_INF)
    s_new = new_key_score(kvn[:, 2 * GD:3 * GD])
    m = jnp.maximum(jnp.max(s_all, axis=1, keepdims=True), s_new)
    p_all = jnp.exp(s_all - m)
    p_new = jnp.exp(s_new - m)
    l = jnp.sum(p_all, axis=1, keepdims=True) + p_new
    o_s = new_val(p_new, kvn[:, 3 * GD:4 * GD])
    for i, pg_ref in enumerate(pages):
        o_s = o_s + _dot_nt(p_all[:, i * PAGE_SIZE:(i + 1) * PAGE_SIZE].astype(BF16),
                            pg_ref[0, GD:, :].astype(BF16))
    o_s = o_s / l

    win = win_ref[0]
    i_w = lax.broadcasted_iota(jnp.int32, (1, WINDOW), 1)
    s_w = _dot(qbd, win[0:GD, :].astype(BF16)) - slope * (WINDOW - i_w).astype(F32)
    s_w = jnp.where(i_w >= 1, s_w, NEG_INF)
    s_wn = new_key_score(kvn[:, 4 * GD:5 * GD])
    m = jnp.maximum(jnp.max(s_w, axis=1, keepdims=True), s_wn)
    p_w = jnp.exp(s_w - m)
    p_wn = jnp.exp(s_wn - m)
    l = jnp.sum(p_w, axis=1, keepdims=True) + p_wn
    o_w = (new_val(p_wn, kvn[:, 5 * GD:]) + _dot_nt(p_w.astype(BF16), win[GD:, :].astype(BF16))) / l

    lane_b = lax.broadcasted_iota(jnp.int32, wcol_ref.shape, 1)
    new_col = jnp.sum(jnp.where(lane_b == b, wcol_ref[...], 0.0), axis=1, keepdims=True)
    lane_w = lax.broadcasted_iota(jnp.int32, win.shape, 1)
    wino_ref[0] = jnp.where(lane_w == WINDOW - 1, new_col, pltpu.roll(win, WINDOW - 1, axis=1))

    grow = g_ref[0]
    h128 = lax.broadcasted_iota(jnp.int32, (N_HEADS, LANES), 0)
    c128 = lax.broadcasted_iota(jnp.int32, (N_HEADS, LANES), 1)

    def gate(br):
        return jnp.sum(jnp.where(c128 == 3 * h128 + br, grow, 0.0), axis=1, keepdims=True)

    o = jnp.where(own, gate(0) * o_c + gate(1) * o_s + gate(2) * o_w, 0.0)
    oh = (o[:, 0:HEAD_DIM] + o[:, HEAD_DIM:2 * HEAD_DIM]
          + o[:, 2 * HEAD_DIM:3 * HEAD_DIM] + o[:, 3 * HEAD_DIM:4 * HEAD_DIM])
    o_ref[0] = jnp.concatenate([oh[h:h + 1, :] for h in range(N_HEADS)], axis=1)


def _nsa_sample(page_table, q_s, g_s, kc, vct, kvn, wcol, state_t, cache_t, slopes, ovl, expand):
    per_b = lambda *shape: pl.BlockSpec((1,) + shape, lambda b, pt: (b,) + (0,) * len(shape))
    return pl.pallas_call(
        _nsa_sample_kernel,
        grid_spec=pltpu.PrefetchScalarGridSpec(
            num_scalar_prefetch=1,
            grid=(DEC_BATCH,),
            in_specs=[
                per_b(1, D_MODEL), per_b(1, LANES), per_b(N_CMP_PAD, GD), per_b(GD, N_CMP_PAD),
                per_b(1, KV_ROW + KV_WIN), _const_sp((KV_WIN, DEC_BATCH)), per_b(KV_WIN, WINDOW),
            ] + [_page_spec(p, 1) for p in range(N_PAGES)] + [
                _const_sp((N_HEADS, LANES)), _const_sp((N_CMP_PAD, LANES)), _const_sp((LANES, PAST_LEN)),
            ],
            out_specs=[per_b(1, D_MODEL), per_b(KV_WIN, WINDOW)],
        ),
        out_shape=[
            jax.ShapeDtypeStruct((DEC_BATCH, 1, D_MODEL), F32),
            jax.ShapeDtypeStruct((DEC_BATCH, KV_WIN, WINDOW), F32),
        ],
        compiler_params=_cparams(1),
    )(page_table, q_s, g_s, kc, vct, kvn, wcol, state_t, *([cache_t] * N_PAGES), slopes, ovl, expand)


def _outproj_sample_kernel(o_ref, x_ref, w_ref, xo_ref):
    xo_ref[...] = x_ref[...] + _dot(o_ref[...].astype(BF16), w_ref[...])


def _outproj_sample(o_pad, x, w_out):
    return pl.pallas_call(
        _outproj_sample_kernel,
        grid=(1,),
        in_specs=[
            pl.BlockSpec((TM, D_MODEL), lambda i: (0, 0)),
            pl.BlockSpec((TM, D_MODEL), lambda i: (N_PT, 0)),
            _const((D_MODEL, D_MODEL)),
        ],
        out_specs=pl.BlockSpec((TM, D_MODEL), lambda i: (N_PT, 0)),
        out_shape=jax.ShapeDtypeStruct((N_TOT, D_MODEL), F32),
        input_output_aliases={1: 0},
        compiler_params=_cparams(1),
    )(o_pad, x, w_out)


def _final_norm_kernel(x_ref, nrm_ref, yp_ref, ys_ref):
    i = pl.program_id(0)
    y = _rms(x_ref[...], nrm_ref[...])

    @pl.when(i < N_PT)
    def _():
        yp_ref[...] = y

    @pl.when(i == N_PT)
    def _():
        ys_ref[...] = y[:DEC_BATCH, :]


def _final_norm(x, nrm):
    return pl.pallas_call(
        _final_norm_kernel,
        grid=(N_TILES,),
        in_specs=[pl.BlockSpec((TM, D_MODEL), lambda i: (i, 0)), _const((1, D_MODEL))],
        out_specs=[
            pl.BlockSpec((TM, D_MODEL), lambda i: (jnp.minimum(i, N_PT - 1), 0)),
            pl.BlockSpec((DEC_BATCH, D_MODEL), lambda i: (0, 0)),
        ],
        out_shape=[
            jax.ShapeDtypeStruct((N_P, D_MODEL), F32),
            jax.ShapeDtypeStruct((DEC_BATCH, D_MODEL), F32),
        ],
        compiler_params=_cparams(1),
    )(x, nrm)


def _row(v):
    return v.reshape(1, -1).astype(F32)


def _prep_mix(w_s, b_s):
    causal = jnp.tril(jnp.ones((CHUNK, CHUNK), F32))
    eye = jnp.eye(CHUNK, dtype=F32)
    w0 = w_s * causal
    w1 = w_s[:, 0:1, 0:1] * eye
    b0 = jnp.repeat(b_s.T, D_A // A_GROUPS, axis=1)
    b1 = jnp.broadcast_to(jnp.repeat(b_s[:, 0], D_A // A_GROUPS)[None, :], (CHUNK, D_A))
    return jnp.stack([w0, w1]).astype(BF16), jnp.stack([b0, b1]).astype(F32)


def _prep_compress(cmp_pe, cmp_w1, cmp_w2):
    eye = jnp.eye(N_KV_GROUPS, dtype=F32)
    w1 = cmp_w1.reshape(2, 2, CMP_STRIDE, HEAD_DIM, HEAD_DIM)
    ws = jnp.einsum('khrde,gj->krgdhje', w1, eye).reshape(2, CMP_STRIDE * GD, 2 * GD).astype(BF16)
    cvec = jnp.einsum('kld,klde->ke', cmp_pe, cmp_w1, precision=HIGHEST)
    cvec = jnp.tile(cvec, (1, N_KV_GROUPS)).reshape(2, 1, GD).astype(F32)
    w2bd = jnp.einsum('ked,gj->kgejd', cmp_w2, eye).reshape(2, GD, GD).astype(BF16)
    return ws, cvec, w2bd


def _overlap():
    n = np.arange(N_CMP_PAD)[:, None] * CMP_STRIDE
    s0 = np.arange(LANES)[None, :] * SLC_BLOCK
    ovl = ((n < s0 + SLC_BLOCK) & (n + CMP_LEN > s0)).astype(np.float32)
    ovl[N_CMP_PAD - 1, :] = 0.0
    return ovl


def _position_minor(a):
    return jnp.transpose(a, (0, 2, 3, 4, 1)).reshape(a.shape[0], -1, a.shape[1])


def _token_major(a, n_kinds):
    a = a.reshape(a.shape[0], n_kinds, N_KV_GROUPS, HEAD_DIM, a.shape[-1])
    return jnp.transpose(a, (0, 4, 1, 2, 3))


def kernel(x_prompt, x_sample, cache_kv, state_win_kv, page_table, norm_mix, norm_ffn, norm_final, a_w_in, a_ln_g, a_ln_b, a_w_s, a_b_s, a_w_out, kv_norm, w_kv, cmp_pe, cmp_w1, cmp_w2, b_w_in, b_b_gate, b_w_out, f_w_gate, f_w_up, f_w_down, m_w_router, m_b_router, m_w_gate, m_w_up, m_w_down):
    x = jnp.concatenate([x_prompt.reshape(N_P, D_MODEL), x_sample.reshape(DEC_BATCH, D_MODEL),
                         jnp.zeros((TM - DEC_BATCH, D_MODEL), F32)], axis=0)
    cache_t = _position_minor(cache_kv)
    state_t = _position_minor(state_win_kv)
    ovl = _overlap()
    ovl_s = jnp.asarray(ovl)
    ovl_pt = jnp.asarray(ovl[:, :N_SEL_P].T.copy())
    expand = jnp.asarray((np.arange(PAST_LEN)[None, :] // SLC_BLOCK == np.arange(LANES)[:, None])
                         .astype(np.float32)).astype(BF16)
    slopes = jnp.asarray(np.repeat(np.asarray(_SLOPES, np.float32)[:, None], LANES, axis=1))
    moe_w = [w.astype(BF16) for w in (m_w_gate, m_w_up, m_w_down)]

    v_p, v_s = [], []
    win_s = None
    for layer in range(DEPTH):
        if layer == N_A_LAYERS:
            wkv = w_kv.astype(BF16)
            (rowst_p, wint_p, kvt_s, kvn_s, cmp_nat,
             ksel, kwin, vselt, vwint) = _kvproj(x, _row(kv_norm), wkv, wkv.T)
            ws, cvec, w2bd = _prep_compress(cmp_pe, cmp_w1, cmp_w2)
            kc_p, vct_p = _compress_prompt(cmp_nat, ws, cvec, w2bd)
            kc_s, vct_s = _compress_sample(page_table, cache_t, ws, cvec, w2bd)
        if layer < N_A_LAYERS:
            wmix, bias = _prep_mix(a_w_s[layer], a_b_s[layer])
            x, vp, vs = _a_mixer(x, _row(norm_mix[layer]), a_w_in[layer].astype(BF16), _row(a_ln_g[layer]),
                                 _row(a_ln_b[layer]), wmix, bias, a_w_out[layer].astype(BF16))
            v_p.append(vp)
            v_s.append(vs)
        else:
            i = layer - N_A_LAYERS
            w_in = jnp.pad(b_w_in[i], ((0, 0), (0, LANES - 3 * N_HEADS))).astype(BF16)
            bg = jnp.pad(b_b_gate[i], (0, LANES - 3 * N_HEADS)).reshape(1, LANES).astype(F32)
            w_out = b_w_out[i].astype(BF16)
            q, gates = _qproj(x, _row(norm_mix[layer]), w_in, bg)
            q_s = q[N_P:N_P + DEC_BATCH].reshape(DEC_BATCH, 1, D_MODEL)
            g_s = gates[N_P:N_P + DEC_BATCH].reshape(DEC_BATCH, 1, LANES)
            o_s, win_s = _nsa_sample(page_table, q_s, g_s, kc_s, vct_s,
                                     kvn_s.reshape(DEC_BATCH, 1, KV_ROW + KV_WIN), kvt_s[KV_ROW:],
                                     state_t, cache_t, slopes, ovl_s, expand)
            x = _nsa_prompt(q, gates, x, kc_p, vct_p, ksel, vselt, kwin, vwint, ovl_pt, w_out)
            o_pad = jnp.pad(o_s.reshape(DEC_BATCH, D_MODEL), ((0, TM - DEC_BATCH), (0, 0)))
            x = _outproj_sample(o_pad, x, w_out)
        j = layer // 2
        if layer % 2 == 0:
            x = _ffn(x, _row(norm_ffn[layer]), f_w_gate[j].astype(BF16), f_w_up[j].astype(BF16),
                     f_w_down[j].astype(BF16))
        else:
            wr = jnp.pad(m_w_router[j], ((0, 0), (0, LANES - N_EXPERTS))).astype(F32)
            br = jnp.pad(m_b_router[j], (0, LANES - N_EXPERTS), constant_values=NEG_INF).reshape(1, LANES)
            x = _moe_layer(x, _row(norm_ffn[layer]), wr, br.astype(F32), *moe_w, j)
    y_p, y_s = _final_norm(x, _row(norm_final))

    return (y_p.reshape(BATCH, SEQ, D_MODEL),
            y_s.reshape(DEC_BATCH, 1, D_MODEL),
            _token_major(rowst_p, 4),
            _token_major(kvt_s[:KV_ROW].reshape(1, KV_ROW, DEC_BATCH), 4).reshape(
                DEC_BATCH, 1, 4, N_KV_GROUPS, HEAD_DIM),
            _token_major(wint_p[:, :, SEQ - WINDOW:], 2),
            _token_major(win_s, 2),
            jnp.stack(v_p),
            jnp.stack(v_s).reshape(N_A_LAYERS, DEC_BATCH, 1, D_A))
```

```python
import functools

import numpy as np
import jax
import jax.numpy as jnp
from jax import lax
from jax.experimental import pallas as pl
from jax.experimental.pallas import tpu as pltpu

F32 = jnp.float32
BF16 = jnp.bfloat16
HIGHEST = lax.Precision.HIGHEST

D_MODEL = 1024
BATCH = 8
SEQ = 2048
DEPTH = 4
DEC_BATCH = 128
PAST_LEN = 2048
PAGE_SIZE = 128
N_A_LAYERS = DEPTH // 2
CHUNK = 128
D_A = D_MODEL
A_GROUPS = 8
N_HEADS = 16
HEAD_DIM = 64
N_KV_GROUPS = 4
Q_PER_GROUP = 4
CMP_LEN = 32
CMP_STRIDE = 16
SLC_BLOCK = 64
N_TOP = 16
WINDOW = 512
D_FF = 2816
N_EXPERTS = 8
TOP_K = 2
RMS_EPS = 1e-6
LN_EPS = 1e-5
NEG_INF = -1e30
FORCE_SCORE = 1e4

LANES = 128
TM = 512
N_P = BATCH * SEQ
N_PT = N_P // TM
N_TOT = N_P + TM
N_TILES = N_TOT // TM
N_REAL = N_P + DEC_BATCH
TQ = 128
BK_SEL = 256
BK_WIN = 128
N_WIN_KEYS = WINDOW + TQ
N_CMP_PAD = 128
N_SEL_P = SEQ // SLC_BLOCK
N_PAGES = PAST_LEN // PAGE_SIZE
GD = N_KV_GROUPS * HEAD_DIM
KV_ROW = 4 * GD
KV_WIN = 2 * GD
T_MOE = 512
N_ASSIGN = N_REAL * TOP_K
N_MOE_BLOCKS = -(-N_ASSIGN // T_MOE) + N_EXPERTS
N_MOE_ROWS = N_MOE_BLOCKS * T_MOE
FF_SPLIT = 2
VMEM_LIMIT = 56 * 1024 * 1024

_SLOPES = [float(v) for v in
           (2.0 ** (-8.0 * np.arange(1, N_HEADS + 1, dtype=np.float32) / N_HEADS)).astype(np.float32)]

K_AUG = 2 * HEAD_DIM
AUG_ROWS = 16
AUG_ONE, AUG_HI, AUG_LO, AUG_SEL = 0, 3, 6, 9
POS_LO = 128


def _split3_const(v):
    out = []
    r = np.float32(v)
    for _ in range(3):
        p = np.float32(np.asarray(r, dtype=jnp.bfloat16))
        out.append(float(p))
        r = np.float32(r - p)
    return out


_SLOPE_SPLIT = [_split3_const(v) for v in _SLOPES]


def _cparams(n_axes):
    return pltpu.CompilerParams(dimension_semantics=("arbitrary",) * n_axes,
                                vmem_limit_bytes=VMEM_LIMIT)


def _const(shape):
    nd = len(shape)
    return pl.BlockSpec(shape, lambda *_: (0,) * nd, pipeline_mode=pl.Buffered(1))


def _rms(x, g):
    return x * lax.rsqrt(jnp.mean(x * x, axis=-1, keepdims=True) + RMS_EPS) * g


def _dot(a, b, **kw):
    return jnp.dot(a, b, preferred_element_type=F32, **kw)


def _dot_nt(a, b, **kw):
    return lax.dot_general(a, b, (((1,), (1,)), ((), ())), preferred_element_type=F32, **kw)


def _a_mixer_kernel(x_ref, nrm_ref, win_ref, lng_ref, lnb_ref, wmix_ref, bias_ref, wout_ref,
                    xo_ref, vp_ref, vs_ref, mixed_ref):
    i = pl.program_id(0)
    x = x_ref[...]
    h = _rms(x, nrm_ref[...]).astype(BF16)
    z = jax.nn.gelu(_dot(h, win_ref[...]))
    u = z[:, :D_A]
    v = z[:, D_A:]
    mu = jnp.mean(v, axis=-1, keepdims=True)
    var = jnp.mean(jnp.square(v - mu), axis=-1, keepdims=True)
    v = (v - mu) * lax.rsqrt(var + LN_EPS) * lng_ref[...] + lnb_ref[...]

    @pl.when((i < N_PT) & (i % (SEQ // TM) == SEQ // TM - 1))
    def _():
        vp_ref[0] = v[TM - CHUNK:, :]

    @pl.when(i == N_PT)
    def _():
        vs_ref[...] = v[:DEC_BATCH, :]

    vb = v.astype(BF16)
    for c in range(TM // CHUNK):
        for g in range(A_GROUPS):
            cols = slice(g * LANES, (g + 1) * LANES)
            rows = slice(c * CHUNK, (c + 1) * CHUNK)
            mixed_ref[rows, cols] = _dot(wmix_ref[0, g], vb[rows, cols]) + bias_ref[0, :, cols]
    t = (u * mixed_ref[...]).astype(BF16)
    xo_ref[...] = x + _dot(t, wout_ref[...])


def _a_mixer(x, nrm, w_in, ln_g, ln_b, wmix, bias, w_out):
    return pl.pallas_call(
        _a_mixer_kernel,
        grid=(N_TILES,),
        in_specs=[
            pl.BlockSpec((TM, D_MODEL), lambda i: (i, 0)),
            _const((1, D_MODEL)),
            _const((D_MODEL, 2 * D_A)),
            _const((1, D_A)),
            _const((1, D_A)),
            pl.BlockSpec((1, A_GROUPS, CHUNK, CHUNK), lambda i: (i // N_PT, 0, 0, 0)),
            pl.BlockSpec((1, CHUNK, D_A), lambda i: (i // N_PT, 0, 0)),
            _const((D_A, D_MODEL)),
        ],
        out_specs=[
            pl.BlockSpec((TM, D_MODEL), lambda i: (i, 0)),
            pl.BlockSpec((1, CHUNK, D_A), lambda i: (jnp.minimum(i // (SEQ // TM), BATCH - 1), 0, 0)),
            pl.BlockSpec((DEC_BATCH, D_A), lambda i: (0, 0)),
        ],
        out_shape=[
            jax.ShapeDtypeStruct((N_TOT, D_MODEL), F32),
            jax.ShapeDtypeStruct((BATCH, CHUNK, D_A), F32),
            jax.ShapeDtypeStruct((DEC_BATCH, D_A), F32),
        ],
        scratch_shapes=[pltpu.VMEM((TM, D_A), F32)],
        compiler_params=_cparams(1),
    )(x, nrm, w_in, ln_g, ln_b, wmix, bias, w_out)


def _swiglu_block(h, wg_ref, wu_ref, wd_ref, lead):
    ffh = D_FF // FF_SPLIT
    out = None
    for s in range(FF_SPLIT):
        cols = slice(s * ffh, (s + 1) * ffh)
        g = _dot(h, wg_ref[lead + (slice(None), cols)])
        u = _dot(h, wu_ref[lead + (slice(None), cols)])
        a = (jax.nn.silu(g) * u).astype(BF16)
        part = _dot(a, wd_ref[lead + (cols, slice(None))])
        out = part if out is None else out + part
    return out


def _ffn_kernel(x_ref, nrm_ref, wg_ref, wu_ref, wd_ref, o_ref):
    x = x_ref[...]
    h = _rms(x, nrm_ref[...]).astype(BF16)
    o_ref[...] = x + _swiglu_block(h, wg_ref, wu_ref, wd_ref, ())


def _ffn(x, nrm, wg, wu, wd):
    return pl.pallas_call(
        _ffn_kernel,
        grid=(N_TILES,),
        in_specs=[
            pl.BlockSpec((TM, D_MODEL), lambda i: (i, 0)),
            _const((1, D_MODEL)),
            _const((D_MODEL, D_FF)),
            _const((D_MODEL, D_FF)),
            _const((D_FF, D_MODEL)),
        ],
        out_specs=pl.BlockSpec((TM, D_MODEL), lambda i: (i, 0)),
        out_shape=jax.ShapeDtypeStruct((N_TOT, D_MODEL), F32),
        compiler_params=_cparams(1),
    )(x, nrm, wg, wu, wd)


def _router_kernel(x_ref, nrm_ref, wr_ref, br_ref, h_ref, r_ref):
    h = _rms(x_ref[...], nrm_ref[...])
    h_ref[...] = h
    logits = _dot(h, wr_ref[...], precision=HIGHEST) + br_ref[...]
    lane = lax.broadcasted_iota(jnp.int32, logits.shape, 1).astype(F32)
    big = float(LANES)
    m1 = jnp.max(logits, axis=1, keepdims=True)
    i1 = jnp.min(jnp.where(logits == m1, lane, big), axis=1, keepdims=True)
    l2 = jnp.where(lane == i1, -jnp.inf, logits)
    m2 = jnp.max(l2, axis=1, keepdims=True)
    i2 = jnp.min(jnp.where(l2 == m2, lane, big), axis=1, keepdims=True)
    e = jnp.exp(m2 - m1)
    g1 = 1.0 / (1.0 + e)
    g2 = e / (1.0 + e)
    r_ref[...] = jnp.where(lane == 0.0, i1, jnp.where(lane == 1.0, i2,
                           jnp.where(lane == 2.0, g1, jnp.where(lane == 3.0, g2, 0.0))))


def _router(x, nrm, wr, br):
    return pl.pallas_call(
        _router_kernel,
        grid=(N_TILES,),
        in_specs=[
            pl.BlockSpec((TM, D_MODEL), lambda i: (i, 0)),
            _const((1, D_MODEL)),
            _const((D_MODEL, LANES)),
            _const((1, LANES)),
        ],
        out_specs=[
            pl.BlockSpec((TM, D_MODEL), lambda i: (i, 0)),
            pl.BlockSpec((TM, LANES), lambda i: (i, 0)),
        ],
        out_shape=[
            jax.ShapeDtypeStruct((N_TOT, D_MODEL), F32),
            jax.ShapeDtypeStruct((N_TOT, LANES), F32),
        ],
        compiler_params=_cparams(1),
    )(x, nrm, wr, br)


def _moe_kernel(be_ref, na_ref, rt_ref, h_hbm, wg_ref, wu_ref, wd_ref, o_ref, xbuf, sem):
    i = pl.program_id(0)
    na = na_ref[0]
    slot = i % 2

    def start_gather(blk, s):
        base = blk * T_MOE

        def body(r, carry):
            pltpu.make_async_copy(h_hbm.at[rt_ref[base + r]], xbuf.at[s, r], sem.at[s]).start()
            return carry

        lax.fori_loop(0, T_MOE, body, 0, unroll=8)

    @pl.when(i == 0)
    def _():
        start_gather(0, 0)

    @pl.when(i + 1 < na)
    def _():
        start_gather(i + 1, 1 - slot)

    @pl.when(i < na)
    def _():
        pltpu.make_async_copy(h_hbm.at[pl.ds(0, T_MOE)], xbuf.at[slot], sem.at[slot]).wait()
        x = jnp.concatenate([xbuf[slot, :, c, :] for c in range(D_MODEL // LANES)], axis=1).astype(BF16)
        o_ref[...] = _swiglu_block(x, wg_ref, wu_ref, wd_ref, (0, 0))

    @pl.when(i >= na)
    def _():
        o_ref[...] = jnp.zeros(o_ref.shape, o_ref.dtype)


def _moe_experts(blk_expert, n_active, row_token, h_tiles, wg, wu, wd, layer):
    return pl.pallas_call(
        _moe_kernel,
        grid_spec=pltpu.PrefetchScalarGridSpec(
            num_scalar_prefetch=3,
            grid=(N_MOE_BLOCKS,),
            in_specs=[
                pl.BlockSpec(memory_space=pl.ANY),
                pl.BlockSpec((1, 1, D_MODEL, D_FF), lambda i, be, na, rt: (layer, be[i], 0, 0)),
                pl.BlockSpec((1, 1, D_MODEL, D_FF), lambda i, be, na, rt: (layer, be[i], 0, 0)),
                pl.BlockSpec((1, 1, D_FF, D_MODEL), lambda i, be, na, rt: (layer, be[i], 0, 0)),
            ],
            out_specs=pl.BlockSpec((T_MOE, D_MODEL), lambda i, be, na, rt: (i, 0)),
            scratch_shapes=[
                pltpu.VMEM((2, T_MOE, D_MODEL // LANES, LANES), F32),
                pltpu.SemaphoreType.DMA((2,)),
            ],
        ),
        out_shape=jax.ShapeDtypeStruct((N_MOE_ROWS, D_MODEL), F32),
        compiler_params=_cparams(1),
    )(blk_expert, n_active, row_token, h_tiles, wg, wu, wd)


def _moe_layer(x, nrm, wr, br, wg, wu, wd, layer):
    h, r = _router(x, nrm, wr, br)
    r = r[:N_REAL]
    expert = r[:, 0:2].astype(jnp.int32).reshape(N_ASSIGN)
    gate = r[:, 2:4]
    onehot = (expert[:, None] == jnp.arange(N_EXPERTS, dtype=jnp.int32)[None, :]).astype(jnp.int32)
    csum = jnp.cumsum(onehot, axis=0)
    counts = csum[-1]
    rank = jnp.sum(csum * onehot, axis=1) - 1
    padded = (counts + T_MOE - 1) // T_MOE * T_MOE
    pad_end = jnp.cumsum(padded)
    pad_start = pad_end - padded
    start = jnp.cumsum(counts) - counts
    dest = jnp.sum(pad_start[None, :] * onehot, axis=1) + rank
    blk_start = jnp.arange(N_MOE_BLOCKS, dtype=jnp.int32) * T_MOE
    blk_expert = jnp.minimum(jnp.sum(pad_end[None, :] <= blk_start[:, None], axis=1),
                             N_EXPERTS - 1).astype(jnp.int32)
    n_active = (pad_end[-1:] // T_MOE).astype(jnp.int32)
    order = jnp.argsort(expert, stable=True).astype(jnp.int32)
    rows = jnp.arange(N_MOE_ROWS, dtype=jnp.int32)
    e_row = blk_expert[rows // T_MOE]
    k = rows - pad_start[e_row]
    src = jnp.clip(start[e_row] + k, 0, N_ASSIGN - 1)
    row_token = jnp.where(k < counts[e_row], order[src] // TOP_K, N_REAL).astype(jnp.int32)
    h_tiles = h.reshape(N_TOT, D_MODEL // LANES, LANES)
    y_rows = _moe_experts(blk_expert, n_active, row_token, h_tiles, wg, wu, wd, layer)
    d2 = dest.reshape(N_REAL, TOP_K)
    y = gate[:, 0:1] * y_rows[d2[:, 0]] + gate[:, 1:2] * y_rows[d2[:, 1]]
    return x.at[:N_REAL].add(y)


def _kvproj_kernel(x_ref, nrm_ref, wkv_ref, wkvt_ref, rowst_ref, wint_ref, kvt_s_ref, kvn_s_ref, cmp_ref,
                   ksel_ref, kwin_ref, vselt_ref, vwint_ref):
    i = pl.program_id(0)
    h = _rms(x_ref[...], nrm_ref[...]).astype(BF16)
    kv = _dot(h, wkv_ref[...])
    kvt = _dot_nt(wkvt_ref[...], h)

    @pl.when(i < N_PT)
    def _():
        rowst_ref[0] = kvt[:KV_ROW, :]
        wint_ref[0] = kvt[KV_ROW:, :]
        for cb in range(2 * GD // LANES):
            cmp_ref[cb] = kv[:, cb * LANES:(cb + 1) * LANES]

    @pl.when(i == N_PT)
    def _():
        kvt_s_ref[...] = kvt[:, :DEC_BATCH]
        kvn_s_ref[...] = kv[:DEC_BATCH, :]

    kvt_b = kvt.astype(BF16)
    pos = (i % (SEQ // TM)) * TM + lax.broadcasted_iota(jnp.int32, (TM, HEAD_DIM), 0)
    a_col = lax.broadcasted_iota(jnp.int32, (TM, HEAD_DIM), 1)
    hi = (pos // POS_LO * POS_LO).astype(F32)
    lo = (pos % POS_LO).astype(F32)
    aug = jnp.where(a_col < AUG_HI, 1.0, jnp.where(a_col < AUG_LO, hi, jnp.where(a_col < AUG_SEL, lo, 0.0)))
    sel_hot = jnp.where((a_col >= AUG_SEL) & (a_col - AUG_SEL == (pos // SLC_BLOCK) % (BK_SEL // SLC_BLOCK)),
                        1.0, 0.0)
    for g in range(N_KV_GROUPS):
        k_s = kv[:, 2 * GD + g * HEAD_DIM:2 * GD + (g + 1) * HEAD_DIM]
        k_w = kv[:, 4 * GD + g * HEAD_DIM:4 * GD + (g + 1) * HEAD_DIM]
        ksel_ref[g] = jnp.concatenate([k_s, aug + sel_hot], axis=1).astype(BF16)
        kwin_ref[g] = jnp.concatenate([k_w, aug], axis=1).astype(BF16)
        for j in range(TM // BK_SEL):
            vselt_ref[g, j] = kvt_b[3 * GD + g * HEAD_DIM:3 * GD + (g + 1) * HEAD_DIM,
                                    j * BK_SEL:(j + 1) * BK_SEL]
        for j in range(TM // BK_WIN):
            vwint_ref[g, j] = kvt_b[5 * GD + g * HEAD_DIM:5 * GD + (g + 1) * HEAD_DIM,
                                    j * BK_WIN:(j + 1) * BK_WIN]


def _kvproj(x, nrm, wkv, wkvt):
    n_kv = KV_ROW + KV_WIN
    tiles_per_seq = SEQ // TM
    ip = lambda i: jnp.minimum(i, N_PT - 1)
    return pl.pallas_call(
        _kvproj_kernel,
        grid=(N_TILES,),
        in_specs=[
            pl.BlockSpec((TM, D_MODEL), lambda i: (i, 0)),
            _const((1, D_MODEL)),
            _const((D_MODEL, n_kv)),
            _const((n_kv, D_MODEL)),
        ],
        out_specs=[
            pl.BlockSpec((1, KV_ROW, TM), lambda i: (ip(i) // tiles_per_seq, 0, ip(i) % tiles_per_seq)),
            pl.BlockSpec((1, KV_WIN, TM), lambda i: (ip(i) // tiles_per_seq, 0, ip(i) % tiles_per_seq)),
            pl.BlockSpec((n_kv, DEC_BATCH), lambda i: (0, 0)),
            pl.BlockSpec((DEC_BATCH, n_kv), lambda i: (0, 0)),
            pl.BlockSpec((2 * GD // LANES, TM, LANES), lambda i: (0, ip(i), 0)),
            pl.BlockSpec((N_KV_GROUPS, TM, K_AUG), lambda i: (0, i, 0)),
            pl.BlockSpec((N_KV_GROUPS, TM, K_AUG), lambda i: (0, i, 0)),
            pl.BlockSpec((N_KV_GROUPS, TM // BK_SEL, HEAD_DIM, BK_SEL), lambda i: (0, i, 0, 0)),
            pl.BlockSpec((N_KV_GROUPS, TM // BK_WIN, HEAD_DIM, BK_WIN), lambda i: (0, i, 0, 0)),
        ],
        out_shape=[
            jax.ShapeDtypeStruct((BATCH, KV_ROW, SEQ), F32),
            jax.ShapeDtypeStruct((BATCH, KV_WIN, SEQ), F32),
            jax.ShapeDtypeStruct((n_kv, DEC_BATCH), F32),
            jax.ShapeDtypeStruct((DEC_BATCH, n_kv), F32),
            jax.ShapeDtypeStruct((2 * GD // LANES, N_P, LANES), F32),
            jax.ShapeDtypeStruct((N_KV_GROUPS, N_TOT, K_AUG), BF16),
            jax.ShapeDtypeStruct((N_KV_GROUPS, N_TOT, K_AUG), BF16),
            jax.ShapeDtypeStruct((N_KV_GROUPS, N_TOT // BK_SEL, HEAD_DIM, BK_SEL), BF16),
            jax.ShapeDtypeStruct((N_KV_GROUPS, N_TOT // BK_WIN, HEAD_DIM, BK_WIN), BF16),
        ],
        compiler_params=_cparams(1),
    )(x, nrm, wkv, wkvt)


def _compress_body(nat_ref, lead, ws_ref, c_ref, w2_ref, kc_ref, vct_ref):
    n_seg = SEQ // CMP_STRIDE
    for kind in range(2):
        y = None
        for r in range(CMP_STRIDE):
            xr = jnp.concatenate(
                [nat_ref[lead + (kind * (GD // LANES) + j, pl.ds(r, n_seg, stride=CMP_STRIDE), slice(None))]
                 for j in range(GD // LANES)], axis=1)
            part = _dot(xr.astype(BF16), ws_ref[kind, r * GD:(r + 1) * GD, :])
            y = part if y is None else y + part
        a = y[:, :GD]
        b = pltpu.roll(y[:, GD:], N_CMP_PAD - 1, axis=0)
        hid = jax.nn.gelu(a + b + c_ref[kind])
        out = _dot(hid.astype(BF16), w2_ref[kind])
        if kind == 0:
            kc_ref[0] = out.astype(BF16)
        else:
            vct_ref[0] = out.T.astype(BF16)


def _compress_prompt_kernel(cmp_ref, ws_ref, c_ref, w2_ref, kc_ref, vct_ref):
    _compress_body(cmp_ref, (), ws_ref, c_ref, w2_ref, kc_ref, vct_ref)


def _compress_sample_kernel(pt_ref, *refs):
    pages = refs[:N_PAGES]
    ws_ref, c_ref, w2_ref, kc_ref, vct_ref, nat_ref = refs[N_PAGES:]
    for p, page in enumerate(pages):
        for cb in range(2 * GD // LANES):
            nat_ref[cb, p * PAGE_SIZE:(p + 1) * PAGE_SIZE, :] = page[0, cb * LANES:(cb + 1) * LANES, :].T
    _compress_body(nat_ref, (), ws_ref, c_ref, w2_ref, kc_ref, vct_ref)


_CMP_W_SHAPES = [(2, CMP_STRIDE * GD, 2 * GD), (2, 1, GD), (2, GD, GD)]


def _compress_prompt(cmp_nat, ws, cvec, w2bd):
    return pl.pallas_call(
        _compress_prompt_kernel,
        grid=(BATCH,),
        in_specs=[pl.BlockSpec((2 * GD // LANES, SEQ, LANES), lambda b: (0, b, 0))]
        + [_const(s) for s in _CMP_W_SHAPES],
        out_specs=[
            pl.BlockSpec((1, N_CMP_PAD, GD), lambda b: (b, 0, 0)),
            pl.BlockSpec((1, GD, N_CMP_PAD), lambda b: (b, 0, 0)),
        ],
        out_shape=[
            jax.ShapeDtypeStruct((BATCH, N_CMP_PAD, GD), BF16),
            jax.ShapeDtypeStruct((BATCH, GD, N_CMP_PAD), BF16),
        ],
        compiler_params=_cparams(1),
    )(cmp_nat, ws, cvec, w2bd)


def _page_spec(p, half):
    return pl.BlockSpec((1, 2 * GD, PAGE_SIZE), lambda b, pt: (pt[b, p], half, 0))


def _const_sp(shape):
    return pl.BlockSpec(shape, functools.partial(lambda nd, b, pt: (0,) * nd, len(shape)),
                        pipeline_mode=pl.Buffered(1))


def _compress_sample(page_table, cache_t, ws, cvec, w2bd):
    return pl.pallas_call(
        _compress_sample_kernel,
        grid_spec=pltpu.PrefetchScalarGridSpec(
            num_scalar_prefetch=1,
            grid=(DEC_BATCH,),
            in_specs=[_page_spec(p, 0) for p in range(N_PAGES)] + [_const_sp(s) for s in _CMP_W_SHAPES],
            out_specs=[
                pl.BlockSpec((1, N_CMP_PAD, GD), lambda b, pt: (b, 0, 0)),
                pl.BlockSpec((1, GD, N_CMP_PAD), lambda b, pt: (b, 0, 0)),
            ],
            scratch_shapes=[pltpu.VMEM((2 * GD // LANES, PAST_LEN, LANES), F32)],
        ),
        out_shape=[
            jax.ShapeDtypeStruct((DEC_BATCH, N_CMP_PAD, GD), BF16),
            jax.ShapeDtypeStruct((DEC_BATCH, GD, N_CMP_PAD), BF16),
        ],
        compiler_params=_cparams(1),
    )(page_table, *([cache_t] * N_PAGES), ws, cvec, w2bd)


def _qproj_kernel(x_ref, nrm_ref, w_ref, bg_ref, q_ref, g_ref):
    h = _rms(x_ref[...], nrm_ref[...]).astype(BF16)
    p = _dot(h, w_ref[...])
    q_ref[...] = p[:, :D_MODEL] * (HEAD_DIM ** -0.5)
    g_ref[...] = jax.nn.sigmoid(p[:, D_MODEL:] + bg_ref[...])


def _qproj(x, nrm, w, bg):
    return pl.pallas_call(
        _qproj_kernel,
        grid=(N_TILES,),
        in_specs=[
            pl.BlockSpec((TM, D_MODEL), lambda i: (i, 0)),
            _const((1, D_MODEL)),
            _const((D_MODEL, D_MODEL + LANES)),
            _const((1, LANES)),
        ],
        out_specs=[
            pl.BlockSpec((TM, D_MODEL), lambda i: (i, 0)),
            pl.BlockSpec((TM, LANES), lambda i: (i, 0)),
        ],
        out_shape=[
            jax.ShapeDtypeStruct((N_TOT, D_MODEL), F32),
            jax.ShapeDtypeStruct((N_TOT, LANES), F32),
        ],
        compiler_params=_cparams(1),
    )(x, nrm, w, bg)


def _top_mask(score, n_sel, idx, axis):
    cnt = jnp.zeros(score.shape, F32)
    for i in range(n_sel):
        row = score[i:i + 1, :] if axis == 0 else score[:, i:i + 1]
        beats = (row > score) | ((row == score) & (idx > i))
        cnt = cnt + jnp.where(beats, 1.0, 0.0)
    return jnp.where(cnt < float(N_TOP), 0.0, NEG_INF)


def _split3(a):
    a1 = a.astype(BF16).astype(F32)
    r1 = a - a1
    a2 = r1.astype(BF16).astype(F32)
    a3 = (r1 - a2).astype(BF16).astype(F32)
    return a1, a2, a3


def _nsa_prompt_kernel(q_ref, gt_ref, x_ref, kc_ref, vct_ref, ks_ref, vst_ref, kw_ref, vwt_ref,
                       ovl_ref, wout_ref, o_ref, qgt_ref, aug_ref, st_ref, ot_ref, selneg_ref, m_ref, l_ref,
                       acc_ref):
    qi = pl.program_id(1)
    t0 = qi * TQ
    qt_all = q_ref[...].T
    gt = gt_ref[...].T
    tq = t0 + lax.broadcasted_iota(jnp.int32, (1, TQ), 1)
    tq_f = tq.astype(F32)
    groups = range(N_KV_GROUPS)
    heads = range(Q_PER_GROUP)
    lanes = [slice(r * TQ, (r + 1) * TQ) for r in heads]
    wide = Q_PER_GROUP * TQ

    n_idx = lax.broadcasted_iota(jnp.int32, (N_CMP_PAD, 1), 0)
    d_c = tq - (n_idx * CMP_STRIDE + CMP_LEN - 1)
    ok_c = d_c >= 0
    d_cf = d_c.astype(F32)
    j_idx = lax.broadcasted_iota(jnp.int32, (N_SEL_P, 1), 0)
    cur = tq // SLC_BLOCK
    valid = j_idx <= cur
    forced = (j_idx == 0) | (j_idx == cur) | (j_idx == cur - 1)

    rid = lax.broadcasted_iota(jnp.int32, (AUG_ROWS, TQ), 0)
    for g in groups:
        blocks = []
        for r in heads:
            hd = g * Q_PER_GROUP + r
            s1, s2, s3 = _SLOPE_SPLIT[hd]
            a1, a2, a3 = _split3(-_SLOPES[hd] * tq_f)
            slope_rows = jnp.where((rid == AUG_HI) | (rid == AUG_LO), s1,
                                   jnp.where((rid == AUG_HI + 1) | (rid == AUG_LO + 1), s2,
                                             jnp.where((rid == AUG_HI + 2) | (rid == AUG_LO + 2), s3, 0.0)))
            blocks.append(jnp.where(rid == AUG_ONE, a1, jnp.where(rid == AUG_ONE + 1, a2,
                                    jnp.where(rid == AUG_ONE + 2, a3, slope_rows))))
        aug = jnp.concatenate(blocks, axis=1)
        aug_ref[g] = aug
        qgt_ref[g, 0:HEAD_DIM, :] = jnp.concatenate(
            [qt_all[(g * Q_PER_GROUP + r) * HEAD_DIM:(g * Q_PER_GROUP + r + 1) * HEAD_DIM, :]
             for r in heads], axis=1).astype(BF16)
        qgt_ref[g, HEAD_DIM:HEAD_DIM + AUG_ROWS, :] = aug.astype(BF16)
        qgt_ref[g, HEAD_DIM + AUG_ROWS:, :] = jnp.zeros((K_AUG - HEAD_DIM - AUG_ROWS, wide), BF16)

    oc_t = []
    sts_c = [_dot(kc_ref[0, :, g * HEAD_DIM:(g + 1) * HEAD_DIM], qgt_ref[g, 0:HEAD_DIM, :])
             for g in groups]
    for g in groups:
        st = sts_c[g]
        psum = jnp.zeros((N_CMP_PAD, TQ), F32)
        ps = []
        for r in heads:
            s = st[:, lanes[r]] - _SLOPES[g * Q_PER_GROUP + r] * d_cf
            s = jnp.where(ok_c, s, NEG_INF)
            p = jnp.exp(s - jnp.max(s, axis=0, keepdims=True))
            p = p / jnp.sum(p, axis=0, keepdims=True)
            p = jnp.where(ok_c, p, 0.0)
            psum = psum + p
            ps.append(p.astype(BF16))
        oc_t.append(_dot(vct_ref[0, g * HEAD_DIM:(g + 1) * HEAD_DIM, :], jnp.concatenate(ps, axis=1)))
        imp = _dot(ovl_ref[...], psum, precision=HIGHEST)
        score = jnp.where(valid, jnp.where(forced, FORCE_SCORE, imp), NEG_INF)
        selneg_ref[g] = _top_mask(score, N_SEL_P, j_idx, 0)

    m_ref[...] = jnp.full(m_ref.shape, NEG_INF, F32)
    l_ref[...] = jnp.zeros(l_ref.shape, F32)
    acc_ref[...] = jnp.zeros(acc_ref.shape, F32)
    blocks_per_chunk = BK_SEL // SLC_BLOCK
    rid_w = lax.broadcasted_iota(jnp.int32, (AUG_ROWS, wide), 0)

    def score_group(c, g):
        k_rows = pl.ds(pl.multiple_of(c * BK_SEL, BK_SEL), BK_SEL)
        blk = aug_ref[g]
        for jj in range(blocks_per_chunk):
            row = selneg_ref[g, pl.ds(c * blocks_per_chunk + jj, 1), :]
            blk = jnp.where(rid_w == AUG_SEL + jj, jnp.concatenate([row] * Q_PER_GROUP, axis=1), blk)
        qgt_ref[g, HEAD_DIM:HEAD_DIM + AUG_ROWS, :] = blk.astype(BF16)
        st_ref[g] = _dot(ks_ref[g, k_rows, :], qgt_ref[g])

    def sel_chunk(c, diagonal):
        if diagonal:
            causal = (lax.broadcasted_iota(jnp.int32, (BK_SEL, TQ), 1)
                      - lax.broadcasted_iota(jnp.int32, (BK_SEL, TQ), 0) + (t0 - c * BK_SEL)) >= 0
        m_all = m_ref[...]
        l_all = l_ref[...]
        pvs, alpha_all, m_out, l_out = [], [], [], []
        for g in groups:
            ps, alphas, ms, ls = [], [], [], []
            for r in heads:
                s = st_ref[g, :, lanes[r]]
                if diagonal:
                    s = jnp.where(causal, s, NEG_INF)
                m_old = m_all[g, :, lanes[r]]
                m_new = jnp.maximum(m_old, jnp.max(s, axis=0, keepdims=True))
                alpha = jnp.exp(m_old - m_new)
                p = jnp.exp(s - m_new)
                ls.append(alpha * l_all[g, :, lanes[r]] + jnp.sum(p, axis=0, keepdims=True))
                ms.append(m_new)
                ps.append(p.astype(BF16))
                alphas.append(alpha)
            pvs.append(_dot(vst_ref[g, c], jnp.concatenate(ps, axis=1)))
            if not diagonal:
                score_group(c + 1, g)
            alpha_all.append(jnp.concatenate(alphas, axis=1))
            m_out.append(jnp.concatenate(ms, axis=1))
            l_out.append(jnp.concatenate(ls, axis=1))
        for g in groups:
            m_ref[g] = m_out[g]
            l_ref[g] = l_out[g]
            acc_ref[g] = acc_ref[g] * alpha_all[g] + pvs[g]

    def sel_body(c, carry):
        sel_chunk(c, False)
        return carry

    c_last = (t0 + TQ - 1) // BK_SEL
    for g in groups:
        score_group(0, g)
    lax.fori_loop(0, c_last, sel_body, 0)
    sel_chunk(c_last, True)

    for g in groups:
        qgt_ref[g, HEAD_DIM:HEAD_DIM + AUG_ROWS, :] = aug_ref[g].astype(BF16)
    k_start = pl.multiple_of(jnp.maximum(t0 - WINDOW, 0), BK_WIN)
    c_start = k_start // BK_WIN
    d_w = (lax.broadcasted_iota(jnp.int32, (N_WIN_KEYS, TQ), 1)
           - lax.broadcasted_iota(jnp.int32, (N_WIN_KEYS, TQ), 0)) + (t0 - k_start)
    madd_w = jnp.where((d_w >= 0) & (d_w < WINDOW), 0.0, NEG_INF)

    sts_w = [_dot(kw_ref[g, pl.ds(k_start, N_WIN_KEYS), :], qgt_ref[g]) for g in groups]
    for g in groups:
        st = sts_w[g]
        ps = []
        ls = []
        for r in heads:
            s = st[:, lanes[r]] + madd_w
            p = jnp.exp(s - jnp.max(s, axis=0, keepdims=True))
            ls.append(jnp.sum(p, axis=0, keepdims=True))
            ps.append(p.astype(BF16))
        pt = jnp.concatenate(ps, axis=1)
        ow_t = None
        for c in range(N_WIN_KEYS // BK_WIN):
            part = _dot(vwt_ref[g, c_start + c], pt[c * BK_WIN:(c + 1) * BK_WIN, :])
            ow_t = part if ow_t is None else ow_t + part
        ow_t = ow_t / jnp.concatenate(ls, axis=1)
        os_t = acc_ref[g] / l_ref[g]
        for r in heads:
            hd = g * Q_PER_GROUP + r
            ot_ref[hd * HEAD_DIM:(hd + 1) * HEAD_DIM, :] = (
                gt[3 * hd:3 * hd + 1, :] * oc_t[g][:, lanes[r]]
                + gt[3 * hd + 1:3 * hd + 2, :] * os_t[:, lanes[r]]
                + gt[3 * hd + 2:3 * hd + 3, :] * ow_t[:, lanes[r]])

    o = ot_ref[...].T.astype(BF16)
    o_ref[...] = x_ref[...] + _dot(o, wout_ref[...])


def _nsa_prompt(q, gates, x, kc, vct, ksel, vselt, kwin, vwint, ovl_t, w_out):
    nq = SEQ // TQ
    tile = lambda b, qi: (b * nq + qi, 0)
    wide = Q_PER_GROUP * TQ
    return pl.pallas_call(
        _nsa_prompt_kernel,
        grid=(BATCH, nq),
        in_specs=[
            pl.BlockSpec((TQ, D_MODEL), tile),
            pl.BlockSpec((TQ, LANES), tile),
            pl.BlockSpec((TQ, D_MODEL), tile),
            pl.BlockSpec((1, N_CMP_PAD, GD), lambda b, qi: (b, 0, 0)),
            pl.BlockSpec((1, GD, N_CMP_PAD), lambda b, qi: (b, 0, 0)),
            pl.BlockSpec((N_KV_GROUPS, SEQ, K_AUG), lambda b, qi: (0, b, 0)),
            pl.BlockSpec((N_KV_GROUPS, SEQ // BK_SEL, HEAD_DIM, BK_SEL), lambda b, qi: (0, b, 0, 0)),
            pl.BlockSpec((N_KV_GROUPS, SEQ, K_AUG), lambda b, qi: (0, b, 0)),
            pl.BlockSpec((N_KV_GROUPS, SEQ // BK_WIN, HEAD_DIM, BK_WIN), lambda b, qi: (0, b, 0, 0)),
            _const((N_SEL_P, N_CMP_PAD)),
            _const((D_MODEL, D_MODEL)),
        ],
        out_specs=pl.BlockSpec((TQ, D_MODEL), tile),
        out_shape=jax.ShapeDtypeStruct((N_TOT, D_MODEL), F32),
        input_output_aliases={2: 0},
        scratch_shapes=[
            pltpu.VMEM((N_KV_GROUPS, K_AUG, wide), BF16),
            pltpu.VMEM((N_KV_GROUPS, AUG_ROWS, wide), F32),
            pltpu.VMEM((N_KV_GROUPS, BK_SEL, wide), F32),
            pltpu.VMEM((D_MODEL, TQ), F32),
            pltpu.VMEM((N_KV_GROUPS, N_SEL_P, TQ), F32),
            pltpu.VMEM((N_KV_GROUPS, 1, wide), F32),
            pltpu.VMEM((N_KV_GROUPS, 1, wide), F32),
            pltpu.VMEM((N_KV_GROUPS, HEAD_DIM, wide), F32),
        ],
        compiler_params=_cparams(2),
    )(q, gates, x, kc, vct, ksel, vselt, kwin, vwint, ovl_t, w_out)


def _nsa_sample_kernel(pt_ref, q_ref, g_ref, kc_ref, vct_ref, kvn_ref, wcol_ref, win_ref, *refs):
    pages = refs[:N_PAGES]
    slope_ref, ovl_ref, exp_ref, o_ref = refs[N_PAGES:N_PAGES + 4]
    wino_ref = refs[N_PAGES + 4] if len(refs) > N_PAGES + 4 else None
    b = pl.program_id(0)
    t = PAST_LEN
    q = q_ref[0]
    qh = jnp.concatenate([q[:, h * HEAD_DIM:(h + 1) * HEAD_DIM] for h in range(N_HEADS)], axis=0)
    q4 = jnp.concatenate([qh] * N_KV_GROUPS, axis=1)
    hrow = lax.broadcasted_iota(jnp.int32, (N_HEADS, GD), 0)
    col = lax.broadcasted_iota(jnp.int32, (N_HEADS, GD), 1)
    own = (col // HEAD_DIM) == (hrow // Q_PER_GROUP)
    qbd = jnp.where(own, q4, 0.0).astype(BF16)
    qbd_f = qbd.astype(F32)
    slope = slope_ref[:, 0:1]
    kvn = kvn_ref[0]

    def new_key_score(k_new):
        return jnp.sum(qbd_f * k_new.astype(BF16).astype(F32), axis=1, keepdims=True)

    def new_val(p_new, v_new):
        return p_new.astype(BF16).astype(F32) * v_new.astype(BF16).astype(F32)

    lane_c = lax.broadcasted_iota(jnp.int32, (1, N_CMP_PAD), 1)
    d_c = t - (lane_c * CMP_STRIDE + CMP_LEN - 1)
    ok_c = d_c >= 0
    s = _dot_nt(qbd, kc_ref[0]) - slope * d_c.astype(F32)
    s = jnp.where(ok_c, s, NEG_INF)
    p = jnp.exp(s - jnp.max(s, axis=1, keepdims=True))
    p = p / jnp.sum(p, axis=1, keepdims=True)
    p = jnp.where(ok_c, p, 0.0)
    o_c = _dot_nt(p.astype(BF16), vct_ref[0])

    h16r = lax.broadcasted_iota(jnp.int32, (N_HEADS, N_HEADS), 0) // Q_PER_GROUP
    h16c = lax.broadcasted_iota(jnp.int32, (N_HEADS, N_HEADS), 1) // Q_PER_GROUP
    pg = _dot(jnp.where(h16r == h16c, 1.0, 0.0), p, precision=HIGHEST)
    imp = _dot(pg, ovl_ref[...], precision=HIGHEST)
    j_idx = lax.broadcasted_iota(jnp.int32, (1, LANES), 1)
    cur = t // SLC_BLOCK
    forced = (j_idx == 0) | (j_idx == cur) | (j_idx == cur - 1)
    score = jnp.where(j_idx <= cur, jnp.where(forced, FORCE_SCORE, imp), NEG_INF)
    selneg = _top_mask(score, cur + 1, j_idx, 1)
    sel = jnp.where(selneg == 0.0, 1.0, 0.0).astype(BF16)
    selexp = _dot(sel, exp_ref[...])

    s_all = jnp.concatenate([_dot(qbd, pg_ref[0, 0:GD, :].astype(BF16)) for pg_ref in pages], axis=1)
    pos = lax.broadcasted_iota(jnp.int32, (1, PAST_LEN), 1)
    s_all = jnp.where(selexp > 0.5, s_all - slope * (t - pos).astype(F32), NEG_INF)
    s_new = new_key_score(kvn[:, 2 * GD:3 * GD])
    m = jnp.maximum(jnp.max(s_all, axis=1, keepdims=True), s_new)
    p_all = jnp.exp(s_all - m)
    p_new = jnp.exp(s_new - m)
    l = jnp.sum(p_all, axis=1, keepdims=True) + p_new
    o_s = new_val(p_new, kvn[:, 3 * GD:4 * GD])
    for i, pg_ref in enumerate(pages):
        o_s = o_s + _dot_nt(p_all[:, i * PAGE_SIZE:(i + 1) * PAGE_SIZE].astype(BF16),
                            pg_ref[0, GD:, :].astype(BF16))
    o_s = o_s / l

    win = win_ref[0]
    i_w = lax.broadcasted_iota(jnp.int32, (1, WINDOW), 1)
    s_w = _dot(qbd, win[0:GD, :].astype(BF16)) - slope * (WINDOW - i_w).astype(F32)
    s_w = jnp.where(i_w >= 1, s_w, NEG_INF)
    s_wn = new_key_score(kvn[:, 4 * GD:5 * GD])
    m = jnp.maximum(jnp.max(s_w, axis=1, keepdims=True), s_wn)
    p_w = jnp.exp(s_w - m)
    p_wn = jnp.exp(s_wn - m)
    l = jnp.sum(p_w, axis=1, keepdims=True) + p_wn
    o_w = (new_val(p_wn, kvn[:, 5 * GD:]) + _dot_nt(p_w.astype(BF16), win[GD:, :].astype(BF16))) / l

    if wino_ref is not None:
        lane_b = lax.broadcasted_iota(jnp.int32, wcol_ref.shape, 1)
        new_col = jnp.sum(jnp.where(lane_b == b, wcol_ref[...], 0.0), axis=1, keepdims=True)
        lane_w = lax.broadcasted_iota(jnp.int32, win.shape, 1)
        wino_ref[0] = jnp.where(lane_w == WINDOW - 1, new_col, pltpu.roll(win, WINDOW - 1, axis=1))

    grow = g_ref[0]
    h128 = lax.broadcasted_iota(jnp.int32, (N_HEADS, LANES), 0)
    c128 = lax.broadcasted_iota(jnp.int32, (N_HEADS, LANES), 1)

    def gate(br):
        return jnp.sum(jnp.where(c128 == 3 * h128 + br, grow, 0.0), axis=1, keepdims=True)

    o = jnp.where(own, gate(0) * o_c + gate(1) * o_s + gate(2) * o_w, 0.0)
    oh = (o[:, 0:HEAD_DIM] + o[:, HEAD_DIM:2 * HEAD_DIM]
          + o[:, 2 * HEAD_DIM:3 * HEAD_DIM] + o[:, 3 * HEAD_DIM:4 * HEAD_DIM])
    o_ref[0] = jnp.concatenate([oh[h:h + 1, :] for h in range(N_HEADS)], axis=1)


def _nsa_sample(page_table, q_s, g_s, kc, vct, kvn, wcol, state_t, cache_t, slopes, ovl, expand, emit_window):
    per_b = lambda *shape: pl.BlockSpec((1,) + shape, lambda b, pt: (b,) + (0,) * len(shape))
    n_out = 2 if emit_window else 1
    return pl.pallas_call(
        _nsa_sample_kernel,
        grid_spec=pltpu.PrefetchScalarGridSpec(
            num_scalar_prefetch=1,
            grid=(DEC_BATCH,),
            in_specs=[
                per_b(1, D_MODEL), per_b(1, LANES), per_b(N_CMP_PAD, GD), per_b(GD, N_CMP_PAD),
                per_b(1, KV_ROW + KV_WIN), _const_sp((KV_WIN, DEC_BATCH)), per_b(KV_WIN, WINDOW),
            ] + [_page_spec(p, 1) for p in range(N_PAGES)] + [
                _const_sp((N_HEADS, LANES)), _const_sp((N_CMP_PAD, LANES)), _const_sp((LANES, PAST_LEN)),
            ],
            out_specs=[per_b(1, D_MODEL), per_b(KV_WIN, WINDOW)][:n_out],
        ),
        out_shape=[
            jax.ShapeDtypeStruct((DEC_BATCH, 1, D_MODEL), F32),
            jax.ShapeDtypeStruct((DEC_BATCH, KV_WIN, WINDOW), F32),
        ][:n_out],
        compiler_params=_cparams(1),
    )(page_table, q_s, g_s, kc, vct, kvn, wcol, state_t, *([cache_t] * N_PAGES), slopes, ovl, expand)


def _outproj_sample_kernel(o_ref, x_ref, w_ref, xo_ref):
    xo_ref[...] = x_ref[...] + _dot(o_ref[...].astype(BF16), w_ref[...])


def _outproj_sample(o_pad, x, w_out):
    return pl.pallas_call(
        _outproj_sample_kernel,
        grid=(1,),
        in_specs=[
            pl.BlockSpec((TM, D_MODEL), lambda i: (0, 0)),
            pl.BlockSpec((TM, D_MODEL), lambda i: (N_PT, 0)),
            _const((D_MODEL, D_MODEL)),
        ],
        out_specs=pl.BlockSpec((TM, D_MODEL), lambda i: (N_PT, 0)),
        out_shape=jax.ShapeDtypeStruct((N_TOT, D_MODEL), F32),
        input_output_aliases={1: 0},
        compiler_params=_cparams(1),
    )(o_pad, x, w_out)


def _final_norm_kernel(x_ref, nrm_ref, yp_ref, ys_ref):
    i = pl.program_id(0)
    y = _rms(x_ref[...], nrm_ref[...])

    @pl.when(i < N_PT)
    def _():
        yp_ref[...] = y

    @pl.when(i == N_PT)
    def _():
        ys_ref[...] = y[:DEC_BATCH, :]


def _final_norm(x, nrm):
    return pl.pallas_call(
        _final_norm_kernel,
        grid=(N_TILES,),
        in_specs=[pl.BlockSpec((TM, D_MODEL), lambda i: (i, 0)), _const((1, D_MODEL))],
        out_specs=[
            pl.BlockSpec((TM, D_MODEL), lambda i: (jnp.minimum(i, N_PT - 1), 0)),
            pl.BlockSpec((DEC_BATCH, D_MODEL), lambda i: (0, 0)),
        ],
        out_shape=[
            jax.ShapeDtypeStruct((N_P, D_MODEL), F32),
            jax.ShapeDtypeStruct((DEC_BATCH, D_MODEL), F32),
        ],
        compiler_params=_cparams(1),
    )(x, nrm)


def _row(v):
    return v.reshape(1, -1).astype(F32)


def _prep_mix(w_s, b_s):
    causal = jnp.tril(jnp.ones((CHUNK, CHUNK), F32))
    eye = jnp.eye(CHUNK, dtype=F32)
    w0 = w_s * causal
    w1 = w_s[:, 0:1, 0:1] * eye
    b0 = jnp.repeat(b_s.T, D_A // A_GROUPS, axis=1)
    b1 = jnp.broadcast_to(jnp.repeat(b_s[:, 0], D_A // A_GROUPS)[None, :], (CHUNK, D_A))
    return jnp.stack([w0, w1]).astype(BF16), jnp.stack([b0, b1]).astype(F32)


def _prep_compress(cmp_pe, cmp_w1, cmp_w2):
    eye = jnp.eye(N_KV_GROUPS, dtype=F32)
    w1 = cmp_w1.reshape(2, 2, CMP_STRIDE, HEAD_DIM, HEAD_DIM)
    ws = jnp.einsum('khrde,gj->krgdhje', w1, eye).reshape(2, CMP_STRIDE * GD, 2 * GD).astype(BF16)
    cvec = jnp.einsum('kld,klde->ke', cmp_pe, cmp_w1, precision=HIGHEST)
    cvec = jnp.tile(cvec, (1, N_KV_GROUPS)).reshape(2, 1, GD).astype(F32)
    w2bd = jnp.einsum('ked,gj->kgejd', cmp_w2, eye).reshape(2, GD, GD).astype(BF16)
    return ws, cvec, w2bd


def _overlap():
    n = np.arange(N_CMP_PAD)[:, None] * CMP_STRIDE
    s0 = np.arange(LANES)[None, :] * SLC_BLOCK
    ovl = ((n < s0 + SLC_BLOCK) & (n + CMP_LEN > s0)).astype(np.float32)
    ovl[N_CMP_PAD - 1, :] = 0.0
    return ovl


def _position_minor(a):
    return jnp.transpose(a, (0, 2, 3, 4, 1)).reshape(a.shape[0], -1, a.shape[1])


def _token_major(a, n_kinds):
    a = a.reshape(a.shape[0], n_kinds, N_KV_GROUPS, HEAD_DIM, a.shape[-1])
    return jnp.transpose(a, (0, 4, 1, 2, 3))


def kernel(x_prompt, x_sample, cache_kv, state_win_kv, page_table, norm_mix, norm_ffn, norm_final, a_w_in, a_ln_g, a_ln_b, a_w_s, a_b_s, a_w_out, kv_norm, w_kv, cmp_pe, cmp_w1, cmp_w2, b_w_in, b_b_gate, b_w_out, f_w_gate, f_w_up, f_w_down, m_w_router, m_b_router, m_w_gate, m_w_up, m_w_down):
    x = jnp.concatenate([x_prompt.reshape(N_P, D_MODEL), x_sample.reshape(DEC_BATCH, D_MODEL),
                         jnp.zeros((TM - DEC_BATCH, D_MODEL), F32)], axis=0)
    cache_t = _position_minor(cache_kv)
    state_t = _position_minor(state_win_kv)
    ovl = _overlap()
    ovl_s = jnp.asarray(ovl)
    ovl_pt = jnp.asarray(ovl[:, :N_SEL_P].T.copy())
    expand = jnp.asarray((np.arange(PAST_LEN)[None, :] // SLC_BLOCK == np.arange(LANES)[:, None])
                         .astype(np.float32)).astype(BF16)
    slopes = jnp.asarray(np.repeat(np.asarray(_SLOPES, np.float32)[:, None], LANES, axis=1))
    moe_w = [w.astype(BF16) for w in (m_w_gate, m_w_up, m_w_down)]

    v_p, v_s = [], []
    win_s = None
    for layer in range(DEPTH):
        if layer == N_A_LAYERS:
            wkv = w_kv.astype(BF16)
            (rowst_p, wint_p, kvt_s, kvn_s, cmp_nat,
             ksel, kwin, vselt, vwint) = _kvproj(x, _row(kv_norm), wkv, wkv.T)
            ws, cvec, w2bd = _prep_compress(cmp_pe, cmp_w1, cmp_w2)
            kc_p, vct_p = _compress_prompt(cmp_nat, ws, cvec, w2bd)
            kc_s, vct_s = _compress_sample(page_table, cache_t, ws, cvec, w2bd)
        if layer < N_A_LAYERS:
            wmix, bias = _prep_mix(a_w_s[layer], a_b_s[layer])
            x, vp, vs = _a_mixer(x, _row(norm_mix[layer]), a_w_in[layer].astype(BF16), _row(a_ln_g[layer]),
                                 _row(a_ln_b[layer]), wmix, bias, a_w_out[layer].astype(BF16))
            v_p.append(vp)
            v_s.append(vs)
        else:
            i = layer - N_A_LAYERS
            w_in = jnp.pad(b_w_in[i], ((0, 0), (0, LANES - 3 * N_HEADS))).astype(BF16)
            bg = jnp.pad(b_b_gate[i], (0, LANES - 3 * N_HEADS)).reshape(1, LANES).astype(F32)
            w_out = b_w_out[i].astype(BF16)
            q, gates = _qproj(x, _row(norm_mix[layer]), w_in, bg)
            q_s = q[N_P:N_P + DEC_BATCH].reshape(DEC_BATCH, 1, D_MODEL)
            g_s = gates[N_P:N_P + DEC_BATCH].reshape(DEC_BATCH, 1, LANES)
            outs = _nsa_sample(page_table, q_s, g_s, kc_s, vct_s,
                               kvn_s.reshape(DEC_BATCH, 1, KV_ROW + KV_WIN), kvt_s[KV_ROW:],
                               state_t, cache_t, slopes, ovl_s, expand, emit_window=(i == 0))
            o_s = outs[0]
            if i == 0:
                win_s = outs[1]
            x = _nsa_prompt(q, gates, x, kc_p, vct_p, ksel, vselt, kwin, vwint, ovl_pt, w_out)
            o_pad = jnp.pad(o_s.reshape(DEC_BATCH, D_MODEL), ((0, TM - DEC_BATCH), (0, 0)))
            x = _outproj_sample(o_pad, x, w_out)
        j = layer // 2
        if layer % 2 == 0:
            x = _ffn(x, _row(norm_ffn[layer]), f_w_gate[j].astype(BF16), f_w_up[j].astype(BF16),
                     f_w_down[j].astype(BF16))
        else:
            wr = jnp.pad(m_w_router[j], ((0, 0), (0, LANES - N_EXPERTS))).astype(F32)
            br = jnp.pad(m_b_router[j], (0, LANES - N_EXPERTS), constant_values=NEG_INF).reshape(1, LANES)
            x = _moe_layer(x, _row(norm_ffn[layer]), wr, br.astype(F32), *moe_w, j)
    y_p, y_s = _final_norm(x, _row(norm_final))

    return (y_p.reshape(BATCH, SEQ, D_MODEL),
            y_s.reshape(DEC_BATCH, 1, D_MODEL),
            _token_major(rowst_p, 4),
            _token_major(kvt_s[:KV_ROW].reshape(1, KV_ROW, DEC_BATCH), 4).reshape(
                DEC_BATCH, 1, 4, N_KV_GROUPS, HEAD_DIM),
            _token_major(wint_p[:, :, SEQ - WINDOW:], 2),
            _token_major(win_s, 2),
            jnp.stack(v_p),
            jnp.stack(v_s).reshape(N_A_LAYERS, DEC_BATCH, 1, D_A))
```

```python
import functools

import numpy as np
import jax
import jax.numpy as jnp
from jax import lax
from jax.experimental import pallas as pl
from jax.experimental.pallas import tpu as pltpu

F32 = jnp.float32
BF16 = jnp.bfloat16
HIGHEST = lax.Precision.HIGHEST

D_MODEL = 1024
BATCH = 8
SEQ = 2048
DEPTH = 4
DEC_BATCH = 128
PAST_LEN = 2048
PAGE_SIZE = 128
N_A_LAYERS = DEPTH // 2
CHUNK = 128
D_A = D_MODEL
A_GROUPS = 8
N_HEADS = 16
HEAD_DIM = 64
N_KV_GROUPS = 4
Q_PER_GROUP = 4
CMP_LEN = 32
CMP_STRIDE = 16
SLC_BLOCK = 64
N_TOP = 16
WINDOW = 512
D_FF = 2816
N_EXPERTS = 8
TOP_K = 2
RMS_EPS = 1e-6
LN_EPS = 1e-5
NEG_INF = -1e30
FORCE_SCORE = 1e4

LANES = 128
TM = 512
N_P = BATCH * SEQ
N_PT = N_P // TM
N_TOT = N_P + TM
N_TILES = N_TOT // TM
N_REAL = N_P + DEC_BATCH
TQ = 128
BK_SEL = 256
BK_WIN = 128
N_WIN_KEYS = WINDOW + TQ
N_CMP_PAD = 128
N_SEL_P = SEQ // SLC_BLOCK
N_PAGES = PAST_LEN // PAGE_SIZE
GD = N_KV_GROUPS * HEAD_DIM
KV_ROW = 4 * GD
KV_WIN = 2 * GD
T_MOE = 512
N_ASSIGN = N_REAL * TOP_K
N_MOE_BLOCKS = -(-N_ASSIGN // T_MOE) + N_EXPERTS
N_MOE_ROWS = N_MOE_BLOCKS * T_MOE
FF_SPLIT = 2
VMEM_LIMIT = 56 * 1024 * 1024

_SLOPES = [float(v) for v in
           (2.0 ** (-8.0 * np.arange(1, N_HEADS + 1, dtype=np.float32) / N_HEADS)).astype(np.float32)]

K_AUG = 2 * HEAD_DIM
AUG_ROWS = 16
AUG_ONE, AUG_HI, AUG_LO, AUG_SEL = 0, 3, 6, 9
POS_LO = 128


def _split3_const(v):
    out = []
    r = np.float32(v)
    for _ in range(3):
        p = np.float32(np.asarray(r, dtype=jnp.bfloat16))
        out.append(float(p))
        r = np.float32(r - p)
    return out


LOG2E = float(np.log2(np.e))
_SLOPES2 = [float(np.float32(np.float32(v) * np.float32(LOG2E))) for v in _SLOPES]
_SLOPE_SPLIT = [_split3_const(v) for v in _SLOPES2]


def _cparams(n_axes):
    return pltpu.CompilerParams(dimension_semantics=("arbitrary",) * n_axes,
                                vmem_limit_bytes=VMEM_LIMIT)


def _const(shape):
    nd = len(shape)
    return pl.BlockSpec(shape, lambda *_: (0,) * nd, pipeline_mode=pl.Buffered(1))


def _rms(x, g):
    return x * lax.rsqrt(jnp.mean(x * x, axis=-1, keepdims=True) + RMS_EPS) * g


def _dot(a, b, **kw):
    return jnp.dot(a, b, preferred_element_type=F32, **kw)


def _dot_nt(a, b, **kw):
    return lax.dot_general(a, b, (((1,), (1,)), ((), ())), preferred_element_type=F32, **kw)


def _a_mixer_kernel(x_ref, nrm_ref, win_ref, lng_ref, lnb_ref, wmix_ref, bias_ref, wout_ref,
                    xo_ref, vp_ref, vs_ref, mixed_ref):
    i = pl.program_id(0)
    x = x_ref[...]
    h = _rms(x, nrm_ref[...]).astype(BF16)
    z = jax.nn.gelu(_dot(h, win_ref[...]))
    u = z[:, :D_A]
    v = z[:, D_A:]
    mu = jnp.mean(v, axis=-1, keepdims=True)
    var = jnp.mean(jnp.square(v - mu), axis=-1, keepdims=True)
    v = (v - mu) * lax.rsqrt(var + LN_EPS) * lng_ref[...] + lnb_ref[...]

    @pl.when((i < N_PT) & (i % (SEQ // TM) == SEQ // TM - 1))
    def _():
        vp_ref[0] = v[TM - CHUNK:, :]

    @pl.when(i == N_PT)
    def _():
        vs_ref[...] = v[:DEC_BATCH, :]

    vb = v.astype(BF16)
    for c in range(TM // CHUNK):
        for g in range(A_GROUPS):
            cols = slice(g * LANES, (g + 1) * LANES)
            rows = slice(c * CHUNK, (c + 1) * CHUNK)
            mixed_ref[rows, cols] = _dot(wmix_ref[0, g], vb[rows, cols]) + bias_ref[0, :, cols]
    t = (u * mixed_ref[...]).astype(BF16)
    xo_ref[...] = x + _dot(t, wout_ref[...])


def _a_mixer(x, nrm, w_in, ln_g, ln_b, wmix, bias, w_out):
    return pl.pallas_call(
        _a_mixer_kernel,
        grid=(N_TILES,),
        in_specs=[
            pl.BlockSpec((TM, D_MODEL), lambda i: (i, 0)),
            _const((1, D_MODEL)),
            _const((D_MODEL, 2 * D_A)),
            _const((1, D_A)),
            _const((1, D_A)),
            pl.BlockSpec((1, A_GROUPS, CHUNK, CHUNK), lambda i: (i // N_PT, 0, 0, 0)),
            pl.BlockSpec((1, CHUNK, D_A), lambda i: (i // N_PT, 0, 0)),
            _const((D_A, D_MODEL)),
        ],
        out_specs=[
            pl.BlockSpec((TM, D_MODEL), lambda i: (i, 0)),
            pl.BlockSpec((1, CHUNK, D_A), lambda i: (jnp.minimum(i // (SEQ // TM), BATCH - 1), 0, 0)),
            pl.BlockSpec((DEC_BATCH, D_A), lambda i: (0, 0)),
        ],
        out_shape=[
            jax.ShapeDtypeStruct((N_TOT, D_MODEL), F32),
            jax.ShapeDtypeStruct((BATCH, CHUNK, D_A), F32),
            jax.ShapeDtypeStruct((DEC_BATCH, D_A), F32),
        ],
        scratch_shapes=[pltpu.VMEM((TM, D_A), F32)],
        compiler_params=_cparams(1),
    )(x, nrm, w_in, ln_g, ln_b, wmix, bias, w_out)


def _swiglu_block(h, wg_ref, wu_ref, wd_ref, lead):
    ffh = D_FF // FF_SPLIT
    out = None
    for s in range(FF_SPLIT):
        cols = slice(s * ffh, (s + 1) * ffh)
        g = _dot(h, wg_ref[lead + (slice(None), cols)])
        u = _dot(h, wu_ref[lead + (slice(None), cols)])
        a = (jax.nn.silu(g) * u).astype(BF16)
        part = _dot(a, wd_ref[lead + (cols, slice(None))])
        out = part if out is None else out + part
    return out


def _ffn_kernel(x_ref, nrm_ref, wg_ref, wu_ref, wd_ref, o_ref):
    x = x_ref[...]
    h = _rms(x, nrm_ref[...]).astype(BF16)
    o_ref[...] = x + _swiglu_block(h, wg_ref, wu_ref, wd_ref, ())


def _ffn(x, nrm, wg, wu, wd):
    return pl.pallas_call(
        _ffn_kernel,
        grid=(N_TILES,),
        in_specs=[
            pl.BlockSpec((TM, D_MODEL), lambda i: (i, 0)),
            _const((1, D_MODEL)),
            _const((D_MODEL, D_FF)),
            _const((D_MODEL, D_FF)),
            _const((D_FF, D_MODEL)),
        ],
        out_specs=pl.BlockSpec((TM, D_MODEL), lambda i: (i, 0)),
        out_shape=jax.ShapeDtypeStruct((N_TOT, D_MODEL), F32),
        compiler_params=_cparams(1),
    )(x, nrm, wg, wu, wd)


def _router_kernel(x_ref, nrm_ref, wr_ref, br_ref, h_ref, r_ref):
    h = _rms(x_ref[...], nrm_ref[...])
    h_ref[...] = h
    logits = _dot(h, wr_ref[...], precision=HIGHEST) + br_ref[...]
    lane = lax.broadcasted_iota(jnp.int32, logits.shape, 1).astype(F32)
    big = float(LANES)
    m1 = jnp.max(logits, axis=1, keepdims=True)
    i1 = jnp.min(jnp.where(logits == m1, lane, big), axis=1, keepdims=True)
    l2 = jnp.where(lane == i1, -jnp.inf, logits)
    m2 = jnp.max(l2, axis=1, keepdims=True)
    i2 = jnp.min(jnp.where(l2 == m2, lane, big), axis=1, keepdims=True)
    e = jnp.exp(m2 - m1)
    g1 = 1.0 / (1.0 + e)
    g2 = e / (1.0 + e)
    r_ref[...] = jnp.where(lane == 0.0, i1, jnp.where(lane == 1.0, i2,
                           jnp.where(lane == 2.0, g1, jnp.where(lane == 3.0, g2, 0.0))))


def _router(x, nrm, wr, br):
    return pl.pallas_call(
        _router_kernel,
        grid=(N_TILES,),
        in_specs=[
            pl.BlockSpec((TM, D_MODEL), lambda i: (i, 0)),
            _const((1, D_MODEL)),
            _const((D_MODEL, LANES)),
            _const((1, LANES)),
        ],
        out_specs=[
            pl.BlockSpec((TM, D_MODEL), lambda i: (i, 0)),
            pl.BlockSpec((TM, LANES), lambda i: (i, 0)),
        ],
        out_shape=[
            jax.ShapeDtypeStruct((N_TOT, D_MODEL), F32),
            jax.ShapeDtypeStruct((N_TOT, LANES), F32),
        ],
        compiler_params=_cparams(1),
    )(x, nrm, wr, br)


def _moe_kernel(be_ref, na_ref, rt_ref, h_hbm, wg_ref, wu_ref, wd_ref, o_ref, xbuf, sem):
    i = pl.program_id(0)
    na = na_ref[0]
    slot = i % 2

    def start_gather(blk, s):
        base = blk * T_MOE

        def body(r, carry):
            pltpu.make_async_copy(h_hbm.at[rt_ref[base + r]], xbuf.at[s, r], sem.at[s]).start()
            return carry

        lax.fori_loop(0, T_MOE, body, 0, unroll=8)

    @pl.when(i == 0)
    def _():
        start_gather(0, 0)

    @pl.when(i + 1 < na)
    def _():
        start_gather(i + 1, 1 - slot)

    @pl.when(i < na)
    def _():
        pltpu.make_async_copy(h_hbm.at[pl.ds(0, T_MOE)], xbuf.at[slot], sem.at[slot]).wait()
        x = jnp.concatenate([xbuf[slot, :, c, :] for c in range(D_MODEL // LANES)], axis=1).astype(BF16)
        o_ref[...] = _swiglu_block(x, wg_ref, wu_ref, wd_ref, (0, 0))

    @pl.when(i >= na)
    def _():
        o_ref[...] = jnp.zeros(o_ref.shape, o_ref.dtype)


def _moe_experts(blk_expert, n_active, row_token, h_tiles, wg, wu, wd, layer):
    return pl.pallas_call(
        _moe_kernel,
        grid_spec=pltpu.PrefetchScalarGridSpec(
            num_scalar_prefetch=3,
            grid=(N_MOE_BLOCKS,),
            in_specs=[
                pl.BlockSpec(memory_space=pl.ANY),
                pl.BlockSpec((1, 1, D_MODEL, D_FF), lambda i, be, na, rt: (layer, be[i], 0, 0)),
                pl.BlockSpec((1, 1, D_MODEL, D_FF), lambda i, be, na, rt: (layer, be[i], 0, 0)),
                pl.BlockSpec((1, 1, D_FF, D_MODEL), lambda i, be, na, rt: (layer, be[i], 0, 0)),
            ],
            out_specs=pl.BlockSpec((T_MOE, D_MODEL), lambda i, be, na, rt: (i, 0)),
            scratch_shapes=[
                pltpu.VMEM((2, T_MOE, D_MODEL // LANES, LANES), F32),
                pltpu.SemaphoreType.DMA((2,)),
            ],
        ),
        out_shape=jax.ShapeDtypeStruct((N_MOE_ROWS, D_MODEL), F32),
        compiler_params=_cparams(1),
    )(blk_expert, n_active, row_token, h_tiles, wg, wu, wd)


def _moe_layer(x, nrm, wr, br, wg, wu, wd, layer):
    h, r = _router(x, nrm, wr, br)
    r = r[:N_REAL]
    expert = r[:, 0:2].astype(jnp.int32).reshape(N_ASSIGN)
    gate = r[:, 2:4]
    onehot = (expert[:, None] == jnp.arange(N_EXPERTS, dtype=jnp.int32)[None, :]).astype(jnp.int32)
    csum = jnp.cumsum(onehot, axis=0)
    counts = csum[-1]
    rank = jnp.sum(csum * onehot, axis=1) - 1
    padded = (counts + T_MOE - 1) // T_MOE * T_MOE
    pad_end = jnp.cumsum(padded)
    pad_start = pad_end - padded
    dest = jnp.sum(pad_start[None, :] * onehot, axis=1) + rank
    blk_start = jnp.arange(N_MOE_BLOCKS, dtype=jnp.int32) * T_MOE
    blk_expert = jnp.minimum(jnp.sum(pad_end[None, :] <= blk_start[:, None], axis=1),
                             N_EXPERTS - 1).astype(jnp.int32)
    n_active = (pad_end[-1:] // T_MOE).astype(jnp.int32)
    n_pad = N_MOE_ROWS - N_ASSIGN
    gaps = padded - counts
    gap_end = jnp.cumsum(gaps)
    gap_start = gap_end - gaps
    p = jnp.arange(n_pad, dtype=jnp.int32)
    e_p = jnp.sum(gap_end[None, :] <= p[:, None], axis=1)
    hot_p = (e_p[:, None] == jnp.arange(N_EXPERTS, dtype=jnp.int32)[None, :]).astype(jnp.int32)
    row_in = jnp.sum(hot_p * (pad_start + counts - gap_start)[None, :], axis=1) + p
    pad_rows = jnp.where(e_p < N_EXPERTS, row_in, pad_end[-1] + p - gap_end[-1])
    keys = jnp.concatenate([dest, pad_rows]).astype(jnp.int32)
    vals = jnp.concatenate([jnp.arange(N_ASSIGN, dtype=jnp.int32) // TOP_K, jnp.full((n_pad,), N_REAL, jnp.int32)])
    _, row_token = lax.sort_key_val(keys, vals)
    h_tiles = h.reshape(N_TOT, D_MODEL // LANES, LANES)
    y_rows = _moe_experts(blk_expert, n_active, row_token, h_tiles, wg, wu, wd, layer)
    d2 = dest.reshape(N_REAL, TOP_K)
    y = gate[:, 0:1] * y_rows[d2[:, 0]] + gate[:, 1:2] * y_rows[d2[:, 1]]
    return x.at[:N_REAL].add(y)


def _kvproj_kernel(x_ref, nrm_ref, wkv_ref, wkvt_ref, rowst_ref, wint_ref, kvt_s_ref, kvn_s_ref, cmp_ref,
                   ksel_ref, kwin_ref, vselt_ref, vwint_ref):
    i = pl.program_id(0)
    h = _rms(x_ref[...], nrm_ref[...]).astype(BF16)
    kv = _dot(h, wkv_ref[...])
    kvt = _dot_nt(wkvt_ref[...], h)

    @pl.when(i < N_PT)
    def _():
        rowst_ref[0] = kvt[:KV_ROW, :]
        wint_ref[0] = kvt[KV_ROW:, :]
        for cb in range(2 * GD // LANES):
            cmp_ref[cb] = kv[:, cb * LANES:(cb + 1) * LANES]

    @pl.when(i == N_PT)
    def _():
        kvt_s_ref[...] = kvt[:, :DEC_BATCH]
        kvn_s_ref[...] = kv[:DEC_BATCH, :]

    kvt_b = kvt.astype(BF16)
    pos = (i % (SEQ // TM)) * TM + lax.broadcasted_iota(jnp.int32, (TM, HEAD_DIM), 0)
    a_col = lax.broadcasted_iota(jnp.int32, (TM, HEAD_DIM), 1)
    hi = (pos // POS_LO * POS_LO).astype(F32)
    lo = (pos % POS_LO).astype(F32)
    aug = jnp.where(a_col < AUG_HI, 1.0, jnp.where(a_col < AUG_LO, hi, jnp.where(a_col < AUG_SEL, lo, 0.0)))
    sel_hot = jnp.where((a_col >= AUG_SEL) & (a_col - AUG_SEL == (pos // SLC_BLOCK) % (BK_SEL // SLC_BLOCK)),
                        1.0, 0.0)
    for g in range(N_KV_GROUPS):
        k_s = kv[:, 2 * GD + g * HEAD_DIM:2 * GD + (g + 1) * HEAD_DIM]
        k_w = kv[:, 4 * GD + g * HEAD_DIM:4 * GD + (g + 1) * HEAD_DIM]
        ksel_ref[g] = jnp.concatenate([k_s, aug + sel_hot], axis=1).astype(BF16)
        kwin_ref[g] = jnp.concatenate([k_w, aug], axis=1).astype(BF16)
        for j in range(TM // BK_SEL):
            vselt_ref[g, j] = kvt_b[3 * GD + g * HEAD_DIM:3 * GD + (g + 1) * HEAD_DIM,
                                    j * BK_SEL:(j + 1) * BK_SEL]
        for j in range(TM // BK_WIN):
            vwint_ref[g, j] = kvt_b[5 * GD + g * HEAD_DIM:5 * GD + (g + 1) * HEAD_DIM,
                                    j * BK_WIN:(j + 1) * BK_WIN]


def _kvproj(x, nrm, wkv, wkvt):
    n_kv = KV_ROW + KV_WIN
    tiles_per_seq = SEQ // TM
    ip = lambda i: jnp.minimum(i, N_PT - 1)
    return pl.pallas_call(
        _kvproj_kernel,
        grid=(N_TILES,),
        in_specs=[
            pl.BlockSpec((TM, D_MODEL), lambda i: (i, 0)),
            _const((1, D_MODEL)),
            _const((D_MODEL, n_kv)),
            _const((n_kv, D_MODEL)),
        ],
        out_specs=[
            pl.BlockSpec((1, KV_ROW, TM), lambda i: (ip(i) // tiles_per_seq, 0, ip(i) % tiles_per_seq)),
            pl.BlockSpec((1, KV_WIN, TM), lambda i: (ip(i) // tiles_per_seq, 0, ip(i) % tiles_per_seq)),
            pl.BlockSpec((n_kv, DEC_BATCH), lambda i: (0, 0)),
            pl.BlockSpec((DEC_BATCH, n_kv), lambda i: (0, 0)),
            pl.BlockSpec((2 * GD // LANES, TM, LANES), lambda i: (0, ip(i), 0)),
            pl.BlockSpec((N_KV_GROUPS, TM, K_AUG), lambda i: (0, i, 0)),
            pl.BlockSpec((N_KV_GROUPS, TM, K_AUG), lambda i: (0, i, 0)),
            pl.BlockSpec((N_KV_GROUPS, TM // BK_SEL, HEAD_DIM, BK_SEL), lambda i: (0, i, 0, 0)),
            pl.BlockSpec((N_KV_GROUPS, TM // BK_WIN, HEAD_DIM, BK_WIN), lambda i: (0, i, 0, 0)),
        ],
        out_shape=[
            jax.ShapeDtypeStruct((BATCH, KV_ROW, SEQ), F32),
            jax.ShapeDtypeStruct((BATCH, KV_WIN, SEQ), F32),
            jax.ShapeDtypeStruct((n_kv, DEC_BATCH), F32),
            jax.ShapeDtypeStruct((DEC_BATCH, n_kv), F32),
            jax.ShapeDtypeStruct((2 * GD // LANES, N_P, LANES), F32),
            jax.ShapeDtypeStruct((N_KV_GROUPS, N_TOT, K_AUG), BF16),
            jax.ShapeDtypeStruct((N_KV_GROUPS, N_TOT, K_AUG), BF16),
            jax.ShapeDtypeStruct((N_KV_GROUPS, N_TOT // BK_SEL, HEAD_DIM, BK_SEL), BF16),
            jax.ShapeDtypeStruct((N_KV_GROUPS, N_TOT // BK_WIN, HEAD_DIM, BK_WIN), BF16),
        ],
        compiler_params=_cparams(1),
    )(x, nrm, wkv, wkvt)


def _compress_body(nat_ref, lead, ws_ref, c_ref, w2_ref, kc_ref, vct_ref, prepare=None):
    n_seg = SEQ // CMP_STRIDE
    for kind in range(2):
        if prepare is not None:
            prepare(kind)
        y = None
        for r in range(CMP_STRIDE):
            xr = jnp.concatenate(
                [nat_ref[lead + (kind * (GD // LANES) + j, pl.ds(r, n_seg, stride=CMP_STRIDE), slice(None))]
                 for j in range(GD // LANES)], axis=1)
            part = _dot(xr.astype(BF16), ws_ref[kind, r * GD:(r + 1) * GD, :])
            y = part if y is None else y + part
        a = y[:, :GD]
        b = pltpu.roll(y[:, GD:], N_CMP_PAD - 1, axis=0)
        hid = jax.nn.gelu(a + b + c_ref[kind])
        out = _dot(hid.astype(BF16), w2_ref[kind])
        if kind == 0:
            kc_ref[0] = out.astype(BF16)
        else:
            vct_ref[0] = out.T.astype(BF16)


def _compress_prompt_kernel(cmp_ref, ws_ref, c_ref, w2_ref, kc_ref, vct_ref):
    _compress_body(cmp_ref, (), ws_ref, c_ref, w2_ref, kc_ref, vct_ref)


def _compress_sample_kernel(pt_ref, *refs):
    pages = refs[:N_PAGES]
    ws_ref, c_ref, w2_ref, kc_ref, vct_ref, nat_ref = refs[N_PAGES:]
    def to_token_major(kind):
        for p, page in enumerate(pages):
            for cb in range(kind * (GD // LANES), (kind + 1) * (GD // LANES)):
                nat_ref[cb, p * PAGE_SIZE:(p + 1) * PAGE_SIZE, :] = page[0, cb * LANES:(cb + 1) * LANES, :].T

    _compress_body(nat_ref, (), ws_ref, c_ref, w2_ref, kc_ref, vct_ref, prepare=to_token_major)


_CMP_W_SHAPES = [(2, CMP_STRIDE * GD, 2 * GD), (2, 1, GD), (2, GD, GD)]


def _compress_prompt(cmp_nat, ws, cvec, w2bd):
    return pl.pallas_call(
        _compress_prompt_kernel,
        grid=(BATCH,),
        in_specs=[pl.BlockSpec((2 * GD // LANES, SEQ, LANES), lambda b: (0, b, 0))]
        + [_const(s) for s in _CMP_W_SHAPES],
        out_specs=[
            pl.BlockSpec((1, N_CMP_PAD, GD), lambda b: (b, 0, 0)),
            pl.BlockSpec((1, GD, N_CMP_PAD), lambda b: (b, 0, 0)),
        ],
        out_shape=[
            jax.ShapeDtypeStruct((BATCH, N_CMP_PAD, GD), BF16),
            jax.ShapeDtypeStruct((BATCH, GD, N_CMP_PAD), BF16),
        ],
        compiler_params=_cparams(1),
    )(cmp_nat, ws, cvec, w2bd)


def _page_spec(p, half):
    return pl.BlockSpec((1, 2 * GD, PAGE_SIZE), lambda b, pt: (pt[b, p], half, 0))


def _const_sp(shape):
    return pl.BlockSpec(shape, functools.partial(lambda nd, b, pt: (0,) * nd, len(shape)),
                        pipeline_mode=pl.Buffered(1))


def _compress_sample(page_table, cache_t, ws, cvec, w2bd):
    return pl.pallas_call(
        _compress_sample_kernel,
        grid_spec=pltpu.PrefetchScalarGridSpec(
            num_scalar_prefetch=1,
            grid=(DEC_BATCH,),
            in_specs=[_page_spec(p, 0) for p in range(N_PAGES)] + [_const_sp(s) for s in _CMP_W_SHAPES],
            out_specs=[
                pl.BlockSpec((1, N_CMP_PAD, GD), lambda b, pt: (b, 0, 0)),
                pl.BlockSpec((1, GD, N_CMP_PAD), lambda b, pt: (b, 0, 0)),
            ],
            scratch_shapes=[pltpu.VMEM((2 * GD // LANES, PAST_LEN, LANES), F32)],
        ),
        out_shape=[
            jax.ShapeDtypeStruct((DEC_BATCH, N_CMP_PAD, GD), BF16),
            jax.ShapeDtypeStruct((DEC_BATCH, GD, N_CMP_PAD), BF16),
        ],
        compiler_params=_cparams(1),
    )(page_table, *([cache_t] * N_PAGES), ws, cvec, w2bd)


def _qproj_kernel(x_ref, nrm_ref, w_ref, bg_ref, q_ref, g_ref):
    h = _rms(x_ref[...], nrm_ref[...]).astype(BF16)
    p = _dot(h, w_ref[...])
    q_ref[...] = p[:, :D_MODEL] * (HEAD_DIM ** -0.5)
    g_ref[...] = jax.nn.sigmoid(p[:, D_MODEL:] + bg_ref[...])


def _qproj(x, nrm, w, bg):
    return pl.pallas_call(
        _qproj_kernel,
        grid=(N_TILES,),
        in_specs=[
            pl.BlockSpec((TM, D_MODEL), lambda i: (i, 0)),
            _const((1, D_MODEL)),
            _const((D_MODEL, D_MODEL + LANES)),
            _const((1, LANES)),
        ],
        out_specs=[
            pl.BlockSpec((TM, D_MODEL), lambda i: (i, 0)),
            pl.BlockSpec((TM, LANES), lambda i: (i, 0)),
        ],
        out_shape=[
            jax.ShapeDtypeStruct((N_TOT, D_MODEL), F32),
            jax.ShapeDtypeStruct((N_TOT, LANES), F32),
        ],
        compiler_params=_cparams(1),
    )(x, nrm, w, bg)


def _top_mask(score, n_sel, idx, axis):
    cnt = jnp.zeros(score.shape, F32)
    for i in range(n_sel):
        row = score[i:i + 1, :] if axis == 0 else score[:, i:i + 1]
        beats = (row > score) | ((row == score) & (idx > i))
        cnt = cnt + jnp.where(beats, 1.0, 0.0)
    return jnp.where(cnt < float(N_TOP), 0.0, NEG_INF)


def _split3(a):
    a1 = a.astype(BF16).astype(F32)
    r1 = a - a1
    a2 = r1.astype(BF16).astype(F32)
    a3 = (r1 - a2).astype(BF16).astype(F32)
    return a1, a2, a3


def _nsa_prompt_kernel(q_ref, gt_ref, x_ref, kc_ref, vct_ref, ks_ref, vst_ref, kw_ref, vwt_ref,
                       ovl_ref, wout_ref, o_ref, qgt_ref, aug_ref, st_ref, ot_ref, selneg_ref, m_ref, l_ref,
                       acc_ref):
    qi = pl.program_id(1)
    t0 = qi * TQ
    qt_all = q_ref[...].T
    gt = gt_ref[...].T
    tq = t0 + lax.broadcasted_iota(jnp.int32, (1, TQ), 1)
    tq_f = tq.astype(F32)
    groups = range(N_KV_GROUPS)
    heads = range(Q_PER_GROUP)
    lanes = [slice(r * TQ, (r + 1) * TQ) for r in heads]
    wide = Q_PER_GROUP * TQ

    n_idx = lax.broadcasted_iota(jnp.int32, (N_CMP_PAD, 1), 0)
    d_c = tq - (n_idx * CMP_STRIDE + CMP_LEN - 1)
    ok_c = d_c >= 0
    d_cf = d_c.astype(F32)
    j_idx = lax.broadcasted_iota(jnp.int32, (N_SEL_P, 1), 0)
    cur = tq // SLC_BLOCK
    valid = j_idx <= cur
    forced = (j_idx == 0) | (j_idx == cur) | (j_idx == cur - 1)

    rid = lax.broadcasted_iota(jnp.int32, (AUG_ROWS, TQ), 0)
    qc_t = []
    for g in groups:
        blocks = []
        for r in heads:
            hd = g * Q_PER_GROUP + r
            s1, s2, s3 = _SLOPE_SPLIT[hd]
            a1, a2, a3 = _split3(-_SLOPES2[hd] * tq_f)
            slope_rows = jnp.where((rid == AUG_HI) | (rid == AUG_LO), s1,
                                   jnp.where((rid == AUG_HI + 1) | (rid == AUG_LO + 1), s2,
                                             jnp.where((rid == AUG_HI + 2) | (rid == AUG_LO + 2), s3, 0.0)))
            blocks.append(jnp.where(rid == AUG_ONE, a1, jnp.where(rid == AUG_ONE + 1, a2,
                                    jnp.where(rid == AUG_ONE + 2, a3, slope_rows))))
        aug = jnp.concatenate(blocks, axis=1)
        aug_ref[g] = aug
        qg = jnp.concatenate(
            [qt_all[(g * Q_PER_GROUP + r) * HEAD_DIM:(g * Q_PER_GROUP + r + 1) * HEAD_DIM, :]
             for r in heads], axis=1)
        qc_t.append(qg.astype(BF16))
        qgt_ref[g, 0:HEAD_DIM, :] = (qg * LOG2E).astype(BF16)
        qgt_ref[g, HEAD_DIM:HEAD_DIM + AUG_ROWS, :] = aug.astype(BF16)
        qgt_ref[g, HEAD_DIM + AUG_ROWS:, :] = jnp.zeros((K_AUG - HEAD_DIM - AUG_ROWS, wide), BF16)

    oc_t = []
    sts_c = [_dot(kc_ref[0, :, g * HEAD_DIM:(g + 1) * HEAD_DIM], qc_t[g]) for g in groups]
    for g in groups:
        st = sts_c[g]
        psum = jnp.zeros((N_CMP_PAD, TQ), F32)
        ps = []
        for r in heads:
            s = st[:, lanes[r]] - _SLOPES[g * Q_PER_GROUP + r] * d_cf
            s = jnp.where(ok_c, s, NEG_INF)
            p = jnp.exp(s - jnp.max(s, axis=0, keepdims=True))
            p = p / jnp.sum(p, axis=0, keepdims=True)
            p = jnp.where(ok_c, p, 0.0)
            psum = psum + p
            ps.append(p.astype(BF16))
        oc_t.append(_dot(vct_ref[0, g * HEAD_DIM:(g + 1) * HEAD_DIM, :], jnp.concatenate(ps, axis=1)))
        imp = _dot(ovl_ref[...], psum, precision=HIGHEST)
        score = jnp.where(valid, jnp.where(forced, FORCE_SCORE, imp), NEG_INF)
        selneg_ref[g] = _top_mask(score, N_SEL_P, j_idx, 0)

    m_ref[...] = jnp.full(m_ref.shape, NEG_INF, F32)
    l_ref[...] = jnp.zeros(l_ref.shape, F32)
    acc_ref[...] = jnp.zeros(acc_ref.shape, F32)
    blocks_per_chunk = BK_SEL // SLC_BLOCK
    rid_w = lax.broadcasted_iota(jnp.int32, (AUG_ROWS, wide), 0)

    def score_group(c, g):
        k_rows = pl.ds(pl.multiple_of(c * BK_SEL, BK_SEL), BK_SEL)
        blk = aug_ref[g]
        for jj in range(blocks_per_chunk):
            row = selneg_ref[g, pl.ds(c * blocks_per_chunk + jj, 1), :]
            blk = jnp.where(rid_w == AUG_SEL + jj, jnp.concatenate([row] * Q_PER_GROUP, axis=1), blk)
        qgt_ref[g, HEAD_DIM:HEAD_DIM + AUG_ROWS, :] = blk.astype(BF16)
        st_ref[g] = _dot(ks_ref[g, k_rows, :], qgt_ref[g])

    def sel_chunk(c, diagonal):
        if diagonal:
            causal = (lax.broadcasted_iota(jnp.int32, (BK_SEL, TQ), 1)
                      - lax.broadcasted_iota(jnp.int32, (BK_SEL, TQ), 0) + (t0 - c * BK_SEL)) >= 0
        m_all = m_ref[...]
        l_all = l_ref[...]
        pvs, alpha_all, m_out, l_out = [], [], [], []
        for g in groups:
            ps, alphas, ms, ls = [], [], [], []
            for r in heads:
                s = st_ref[g, :, lanes[r]]
                if diagonal:
                    s = jnp.where(causal, s, NEG_INF)
                m_old = m_all[g, :, lanes[r]]
                m_new = jnp.maximum(m_old, jnp.max(s, axis=0, keepdims=True))
                alpha = jnp.exp2(m_old - m_new)
                p = jnp.exp2(s - m_new)
                ls.append(alpha * l_all[g, :, lanes[r]] + jnp.sum(p, axis=0, keepdims=True))
                ms.append(m_new)
                ps.append(p.astype(BF16))
                alphas.append(alpha)
            pvs.append(_dot(vst_ref[g, c], jnp.concatenate(ps, axis=1)))
            if not diagonal:
                score_group(c + 1, g)
            alpha_all.append(jnp.concatenate(alphas, axis=1))
            m_out.append(jnp.concatenate(ms, axis=1))
            l_out.append(jnp.concatenate(ls, axis=1))
        for g in groups:
            m_ref[g] = m_out[g]
            l_ref[g] = l_out[g]
            acc_ref[g] = acc_ref[g] * alpha_all[g] + pvs[g]

    def sel_body(c, carry):
        sel_chunk(c, False)
        return carry

    c_last = (t0 + TQ - 1) // BK_SEL
    for g in groups:
        score_group(0, g)
    lax.fori_loop(0, c_last, sel_body, 0)
    sel_chunk(c_last, True)

    for g in groups:
        qgt_ref[g, HEAD_DIM:HEAD_DIM + AUG_ROWS, :] = aug_ref[g].astype(BF16)
    k_start = pl.multiple_of(jnp.maximum(t0 - WINDOW, 0), BK_WIN)
    c_start = k_start // BK_WIN
    d_w = (lax.broadcasted_iota(jnp.int32, (N_WIN_KEYS, TQ), 1)
           - lax.broadcasted_iota(jnp.int32, (N_WIN_KEYS, TQ), 0)) + (t0 - k_start)
    madd_w = jnp.where((d_w >= 0) & (d_w < WINDOW), 0.0, NEG_INF)

    sts_w = [_dot(kw_ref[g, pl.ds(k_start, N_WIN_KEYS), :], qgt_ref[g]) for g in groups]
    for g in groups:
        st = sts_w[g]
        ps = []
        ls = []
        for r in heads:
            s = st[:, lanes[r]] + madd_w
            p = jnp.exp2(s - jnp.max(s, axis=0, keepdims=True))
            ls.append(jnp.sum(p, axis=0, keepdims=True))
            ps.append(p.astype(BF16))
        pt = jnp.concatenate(ps, axis=1)
        ow_t = None
        for c in range(N_WIN_KEYS // BK_WIN):
            part = _dot(vwt_ref[g, c_start + c], pt[c * BK_WIN:(c + 1) * BK_WIN, :])
            ow_t = part if ow_t is None else ow_t + part
        ow_t = ow_t / jnp.concatenate(ls, axis=1)
        os_t = acc_ref[g] / l_ref[g]
        for r in heads:
            hd = g * Q_PER_GROUP + r
            ot_ref[hd * HEAD_DIM:(hd + 1) * HEAD_DIM, :] = (
                gt[3 * hd:3 * hd + 1, :] * oc_t[g][:, lanes[r]]
                + gt[3 * hd + 1:3 * hd + 2, :] * os_t[:, lanes[r]]
                + gt[3 * hd + 2:3 * hd + 3, :] * ow_t[:, lanes[r]])

    o = ot_ref[...].T.astype(BF16)
    o_ref[...] = x_ref[...] + _dot(o, wout_ref[...])


def _nsa_prompt(q, gates, x, kc, vct, ksel, vselt, kwin, vwint, ovl_t, w_out):
    nq = SEQ // TQ
    tile = lambda b, qi: (b * nq + qi, 0)
    wide = Q_PER_GROUP * TQ
    return pl.pallas_call(
        _nsa_prompt_kernel,
        grid=(BATCH, nq),
        in_specs=[
            pl.BlockSpec((TQ, D_MODEL), tile),
            pl.BlockSpec((TQ, LANES), tile),
            pl.BlockSpec((TQ, D_MODEL), tile),
            pl.BlockSpec((1, N_CMP_PAD, GD), lambda b, qi: (b, 0, 0)),
            pl.BlockSpec((1, GD, N_CMP_PAD), lambda b, qi: (b, 0, 0)),
            pl.BlockSpec((N_KV_GROUPS, SEQ, K_AUG), lambda b, qi: (0, b, 0)),
            pl.BlockSpec((N_KV_GROUPS, SEQ // BK_SEL, HEAD_DIM, BK_SEL), lambda b, qi: (0, b, 0, 0)),
            pl.BlockSpec((N_KV_GROUPS, SEQ, K_AUG), lambda b, qi: (0, b, 0)),
            pl.BlockSpec((N_KV_GROUPS, SEQ // BK_WIN, HEAD_DIM, BK_WIN), lambda b, qi: (0, b, 0, 0)),
            _const((N_SEL_P, N_CMP_PAD)),
            _const((D_MODEL, D_MODEL)),
        ],
        out_specs=pl.BlockSpec((TQ, D_MODEL), tile),
        out_shape=jax.ShapeDtypeStruct((N_TOT, D_MODEL), F32),
        input_output_aliases={2: 0},
        scratch_shapes=[
            pltpu.VMEM((N_KV_GROUPS, K_AUG, wide), BF16),
            pltpu.VMEM((N_KV_GROUPS, AUG_ROWS, wide), F32),
            pltpu.VMEM((N_KV_GROUPS, BK_SEL, wide), F32),
            pltpu.VMEM((D_MODEL, TQ), F32),
            pltpu.VMEM((N_KV_GROUPS, N_SEL_P, TQ), F32),
            pltpu.VMEM((N_KV_GROUPS, 1, wide), F32),
            pltpu.VMEM((N_KV_GROUPS, 1, wide), F32),
            pltpu.VMEM((N_KV_GROUPS, HEAD_DIM, wide), F32),
        ],
        compiler_params=_cparams(2),
    )(q, gates, x, kc, vct, ksel, vselt, kwin, vwint, ovl_t, w_out)


def _nsa_sample_kernel(pt_ref, q_ref, g_ref, kc_ref, vct_ref, kvn_ref, wcol_ref, win_ref, *refs):
    pages = refs[:N_PAGES]
    slope_ref, ovl_ref, exp_ref, o_ref = refs[N_PAGES:N_PAGES + 4]
    wino_ref = refs[N_PAGES + 4] if len(refs) > N_PAGES + 4 else None
    b = pl.program_id(0)
    t = PAST_LEN
    q = q_ref[0]
    qh = jnp.concatenate([q[:, h * HEAD_DIM:(h + 1) * HEAD_DIM] for h in range(N_HEADS)], axis=0)
    q4 = jnp.concatenate([qh] * N_KV_GROUPS, axis=1)
    hrow = lax.broadcasted_iota(jnp.int32, (N_HEADS, GD), 0)
    col = lax.broadcasted_iota(jnp.int32, (N_HEADS, GD), 1)
    own = (col // HEAD_DIM) == (hrow // Q_PER_GROUP)
    qbd = jnp.where(own, q4, 0.0).astype(BF16)
    qbd_f = qbd.astype(F32)
    slope = slope_ref[:, 0:1]
    kvn = kvn_ref[0]

    def new_key_score(k_new):
        return jnp.sum(qbd_f * k_new.astype(BF16).astype(F32), axis=1, keepdims=True)

    def new_val(p_new, v_new):
        return p_new.astype(BF16).astype(F32) * v_new.astype(BF16).astype(F32)

    lane_c = lax.broadcasted_iota(jnp.int32, (1, N_CMP_PAD), 1)
    d_c = t - (lane_c * CMP_STRIDE + CMP_LEN - 1)
    ok_c = d_c >= 0
    s = _dot_nt(qbd, kc_ref[0]) - slope * d_c.astype(F32)
    s = jnp.where(ok_c, s, NEG_INF)
    p = jnp.exp(s - jnp.max(s, axis=1, keepdims=True))
    p = p / jnp.sum(p, axis=1, keepdims=True)
    p = jnp.where(ok_c, p, 0.0)
    o_c = _dot_nt(p.astype(BF16), vct_ref[0])

    h16r = lax.broadcasted_iota(jnp.int32, (N_HEADS, N_HEADS), 0) // Q_PER_GROUP
    h16c = lax.broadcasted_iota(jnp.int32, (N_HEADS, N_HEADS), 1) // Q_PER_GROUP
    pg = _dot(jnp.where(h16r == h16c, 1.0, 0.0), p, precision=HIGHEST)
    imp = _dot(pg, ovl_ref[...], precision=HIGHEST)
    j_idx = lax.broadcasted_iota(jnp.int32, (1, LANES), 1)
    cur = t // SLC_BLOCK
    forced = (j_idx == 0) | (j_idx == cur) | (j_idx == cur - 1)
    score = jnp.where(j_idx <= cur, jnp.where(forced, FORCE_SCORE, imp), NEG_INF)
    selneg = _top_mask(score, cur + 1, j_idx, 1)
    sel = jnp.where(selneg == 0.0, 1.0, 0.0).astype(BF16)
    selexp = _dot(sel, exp_ref[...])

    s_all = jnp.concatenate([_dot(qbd, pg_ref[0, 0:GD, :].astype(BF16)) for pg_ref in pages], axis=1)
    pos = lax.broadcasted_iota(jnp.int32, (1, PAST_LEN), 1)
    s_all = jnp.where(selexp > 0.5, s_all - slope * (t - pos).astype(F32), NEG_INF)
    s_new = new_key_score(kvn[:, 2 * GD:3 * GD])
    m = jnp.maximum(jnp.max(s_all, axis=1, keepdims=True), s_new)
    p_all = jnp.exp(s_all - m)
    p_new = jnp.exp(s_new - m)
    l = jnp.sum(p_all, axis=1, keepdims=True) + p_new
    o_s = new_val(p_new, kvn[:, 3 * GD:4 * GD])
    for i, pg_ref in enumerate(pages):
        o_s = o_s + _dot_nt(p_all[:, i * PAGE_SIZE:(i + 1) * PAGE_SIZE].astype(BF16),
                            pg_ref[0, GD:, :].astype(BF16))
    o_s = o_s / l

    win = win_ref[0]
    i_w = lax.broadcasted_iota(jnp.int32, (1, WINDOW), 1)
    s_w = _dot(qbd, win[0:GD, :].astype(BF16)) - slope * (WINDOW - i_w).astype(F32)
    s_w = jnp.where(i_w >= 1, s_w, NEG_INF)
    s_wn = new_key_score(kvn[:, 4 * GD:5 * GD])
    m = jnp.maximum(jnp.max(s_w, axis=1, keepdims=True), s_wn)
    p_w = jnp.exp(s_w - m)
    p_wn = jnp.exp(s_wn - m)
    l = jnp.sum(p_w, axis=1, keepdims=True) + p_wn
    o_w = (new_val(p_wn, kvn[:, 5 * GD:]) + _dot_nt(p_w.astype(BF16), win[GD:, :].astype(BF16))) / l

    if wino_ref is not None:
        lane_b = lax.broadcasted_iota(jnp.int32, wcol_ref.shape, 1)
        new_col = jnp.sum(jnp.where(lane_b == b, wcol_ref[...], 0.0), axis=1, keepdims=True)
        lane_w = lax.broadcasted_iota(jnp.int32, win.shape, 1)
        wino_ref[0] = jnp.where(lane_w == WINDOW - 1, new_col, pltpu.roll(win, WINDOW - 1, axis=1))

    grow = g_ref[0]
    h128 = lax.broadcasted_iota(jnp.int32, (N_HEADS, LANES), 0)
    c128 = lax.broadcasted_iota(jnp.int32, (N_HEADS, LANES), 1)

    def gate(br):
        return jnp.sum(jnp.where(c128 == 3 * h128 + br, grow, 0.0), axis=1, keepdims=True)

    o = jnp.where(own, gate(0) * o_c + gate(1) * o_s + gate(2) * o_w, 0.0)
    oh = (o[:, 0:HEAD_DIM] + o[:, HEAD_DIM:2 * HEAD_DIM]
          + o[:, 2 * HEAD_DIM:3 * HEAD_DIM] + o[:, 3 * HEAD_DIM:4 * HEAD_DIM])
    o_ref[0] = jnp.concatenate([oh[h:h + 1, :] for h in range(N_HEADS)], axis=1)


def _nsa_sample(page_table, q_s, g_s, kc, vct, kvn, wcol, state_t, cache_t, slopes, ovl, expand, emit_window):
    per_b = lambda *shape: pl.BlockSpec((1,) + shape, lambda b, pt: (b,) + (0,) * len(shape))
    n_out = 2 if emit_window else 1
    return pl.pallas_call(
        _nsa_sample_kernel,
        grid_spec=pltpu.PrefetchScalarGridSpec(
            num_scalar_prefetch=1,
            grid=(DEC_BATCH,),
            in_specs=[
                per_b(1, D_MODEL), per_b(1, LANES), per_b(N_CMP_PAD, GD), per_b(GD, N_CMP_PAD),
                per_b(1, KV_ROW + KV_WIN), _const_sp((KV_WIN, DEC_BATCH)), per_b(KV_WIN, WINDOW),
            ] + [_page_spec(p, 1) for p in range(N_PAGES)] + [
                _const_sp((N_HEADS, LANES)), _const_sp((N_CMP_PAD, LANES)), _const_sp((LANES, PAST_LEN)),
            ],
            out_specs=[per_b(1, D_MODEL), per_b(KV_WIN, WINDOW)][:n_out],
        ),
        out_shape=[
            jax.ShapeDtypeStruct((DEC_BATCH, 1, D_MODEL), F32),
            jax.ShapeDtypeStruct((DEC_BATCH, KV_WIN, WINDOW), F32),
        ][:n_out],
        compiler_params=_cparams(1),
    )(page_table, q_s, g_s, kc, vct, kvn, wcol, state_t, *([cache_t] * N_PAGES), slopes, ovl, expand)


def _outproj_sample_kernel(o_ref, x_ref, w_ref, xo_ref):
    xo_ref[...] = x_ref[...] + _dot(o_ref[...].astype(BF16), w_ref[...])


def _outproj_sample(o_pad, x, w_out):
    return pl.pallas_call(
        _outproj_sample_kernel,
        grid=(1,),
        in_specs=[
            pl.BlockSpec((TM, D_MODEL), lambda i: (0, 0)),
            pl.BlockSpec((TM, D_MODEL), lambda i: (N_PT, 0)),
            _const((D_MODEL, D_MODEL)),
        ],
        out_specs=pl.BlockSpec((TM, D_MODEL), lambda i: (N_PT, 0)),
        out_shape=jax.ShapeDtypeStruct((N_TOT, D_MODEL), F32),
        input_output_aliases={1: 0},
        compiler_params=_cparams(1),
    )(o_pad, x, w_out)


def _final_norm_kernel(x_ref, nrm_ref, yp_ref, ys_ref):
    i = pl.program_id(0)
    y = _rms(x_ref[...], nrm_ref[...])

    @pl.when(i < N_PT)
    def _():
        yp_ref[...] = y

    @pl.when(i == N_PT)
    def _():
        ys_ref[...] = y[:DEC_BATCH, :]


def _final_norm(x, nrm):
    return pl.pallas_call(
        _final_norm_kernel,
        grid=(N_TILES,),
        in_specs=[pl.BlockSpec((TM, D_MODEL), lambda i: (i, 0)), _const((1, D_MODEL))],
        out_specs=[
            pl.BlockSpec((TM, D_MODEL), lambda i: (jnp.minimum(i, N_PT - 1), 0)),
            pl.BlockSpec((DEC_BATCH, D_MODEL), lambda i: (0, 0)),
        ],
        out_shape=[
            jax.ShapeDtypeStruct((N_P, D_MODEL), F32),
            jax.ShapeDtypeStruct((DEC_BATCH, D_MODEL), F32),
        ],
        compiler_params=_cparams(1),
    )(x, nrm)


def _row(v):
    return v.reshape(1, -1).astype(F32)


def _prep_mix(w_s, b_s):
    causal = jnp.tril(jnp.ones((CHUNK, CHUNK), F32))
    eye = jnp.eye(CHUNK, dtype=F32)
    w0 = w_s * causal
    w1 = w_s[:, 0:1, 0:1] * eye
    b0 = jnp.repeat(b_s.T, D_A // A_GROUPS, axis=1)
    b1 = jnp.broadcast_to(jnp.repeat(b_s[:, 0], D_A // A_GROUPS)[None, :], (CHUNK, D_A))
    return jnp.stack([w0, w1]).astype(BF16), jnp.stack([b0, b1]).astype(F32)


def _prep_compress(cmp_pe, cmp_w1, cmp_w2):
    eye = jnp.eye(N_KV_GROUPS, dtype=F32)
    w1 = cmp_w1.reshape(2, 2, CMP_STRIDE, HEAD_DIM, HEAD_DIM)
    ws = jnp.einsum('khrde,gj->krgdhje', w1, eye).reshape(2, CMP_STRIDE * GD, 2 * GD).astype(BF16)
    cvec = jnp.einsum('kld,klde->ke', cmp_pe, cmp_w1, precision=HIGHEST)
    cvec = jnp.tile(cvec, (1, N_KV_GROUPS)).reshape(2, 1, GD).astype(F32)
    w2bd = jnp.einsum('ked,gj->kgejd', cmp_w2, eye).reshape(2, GD, GD).astype(BF16)
    return ws, cvec, w2bd


def _overlap():
    n = np.arange(N_CMP_PAD)[:, None] * CMP_STRIDE
    s0 = np.arange(LANES)[None, :] * SLC_BLOCK
    ovl = ((n < s0 + SLC_BLOCK) & (n + CMP_LEN > s0)).astype(np.float32)
    ovl[N_CMP_PAD - 1, :] = 0.0
    return ovl


def _position_minor(a):
    return jnp.transpose(a, (0, 2, 3, 4, 1)).reshape(a.shape[0], -1, a.shape[1])


def _token_major(a, n_kinds):
    a = a.reshape(a.shape[0], n_kinds, N_KV_GROUPS, HEAD_DIM, a.shape[-1])
    return jnp.transpose(a, (0, 4, 1, 2, 3))


def kernel(x_prompt, x_sample, cache_kv, state_win_kv, page_table, norm_mix, norm_ffn, norm_final, a_w_in, a_ln_g, a_ln_b, a_w_s, a_b_s, a_w_out, kv_norm, w_kv, cmp_pe, cmp_w1, cmp_w2, b_w_in, b_b_gate, b_w_out, f_w_gate, f_w_up, f_w_down, m_w_router, m_b_router, m_w_gate, m_w_up, m_w_down):
    x = jnp.concatenate([x_prompt.reshape(N_P, D_MODEL), x_sample.reshape(DEC_BATCH, D_MODEL),
                         jnp.zeros((TM - DEC_BATCH, D_MODEL), F32)], axis=0)
    cache_t = _position_minor(cache_kv)
    state_t = _position_minor(state_win_kv)
    ovl = _overlap()
    ovl_s = jnp.asarray(ovl)
    ovl_pt = jnp.asarray(ovl[:, :N_SEL_P].T.copy())
    expand = jnp.asarray((np.arange(PAST_LEN)[None, :] // SLC_BLOCK == np.arange(LANES)[:, None])
                         .astype(np.float32)).astype(BF16)
    slopes = jnp.asarray(np.repeat(np.asarray(_SLOPES, np.float32)[:, None], LANES, axis=1))
    moe_w = [w.astype(BF16) for w in (m_w_gate, m_w_up, m_w_down)]

    v_p, v_s = [], []
    win_s = None
    for layer in range(DEPTH):
        if layer == N_A_LAYERS:
            wkv = w_kv.astype(BF16)
            (rowst_p, wint_p, kvt_s, kvn_s, cmp_nat,
             ksel, kwin, vselt, vwint) = _kvproj(x, _row(kv_norm), wkv, wkv.T)
            ws, cvec, w2bd = _prep_compress(cmp_pe, cmp_w1, cmp_w2)
            kc_p, vct_p = _compress_prompt(cmp_nat, ws, cvec, w2bd)
            kc_s, vct_s = _compress_sample(page_table, cache_t, ws, cvec, w2bd)
        if layer < N_A_LAYERS:
            wmix, bias = _prep_mix(a_w_s[layer], a_b_s[layer])
            x, vp, vs = _a_mixer(x, _row(norm_mix[layer]), a_w_in[layer].astype(BF16), _row(a_ln_g[layer]),
                                 _row(a_ln_b[layer]), wmix, bias, a_w_out[layer].astype(BF16))
            v_p.append(vp)
            v_s.append(vs)
        else:
            i = layer - N_A_LAYERS
            w_in = jnp.pad(b_w_in[i], ((0, 0), (0, LANES - 3 * N_HEADS))).astype(BF16)
            bg = jnp.pad(b_b_gate[i], (0, LANES - 3 * N_HEADS)).reshape(1, LANES).astype(F32)
            w_out = b_w_out[i].astype(BF16)
            q, gates = _qproj(x, _row(norm_mix[layer]), w_in, bg)
            q_s = q[N_P:N_P + DEC_BATCH].reshape(DEC_BATCH, 1, D_MODEL)
            g_s = gates[N_P:N_P + DEC_BATCH].reshape(DEC_BATCH, 1, LANES)
            outs = _nsa_sample(page_table, q_s, g_s, kc_s, vct_s,
                               kvn_s.reshape(DEC_BATCH, 1, KV_ROW + KV_WIN), kvt_s[KV_ROW:],
                               state_t, cache_t, slopes, ovl_s, expand, emit_window=(i == 0))
            o_s = outs[0]
            if i == 0:
                win_s = outs[1]
            x = _nsa_prompt(q, gates, x, kc_p, vct_p, ksel, vselt, kwin, vwint, ovl_pt, w_out)
            o_pad = jnp.pad(o_s.reshape(DEC_BATCH, D_MODEL), ((0, TM - DEC_BATCH), (0, 0)))
            x = _outproj_sample(o_pad, x, w_out)
        j = layer // 2
        if layer % 2 == 0:
            x = _ffn(x, _row(norm_ffn[layer]), f_w_gate[j].astype(BF16), f_w_up[j].astype(BF16),
                     f_w_down[j].astype(BF16))
        else:
            wr = jnp.pad(m_w_router[j], ((0, 0), (0, LANES - N_EXPERTS))).astype(F32)
            br = jnp.pad(m_b_router[j], (0, LANES - N_EXPERTS), constant_values=NEG_INF).reshape(1, LANES)
            x = _moe_layer(x, _row(norm_ffn[layer]), wr, br.astype(F32), *moe_w, j)
    y_p, y_s = _final_norm(x, _row(norm_final))

    return (y_p.reshape(BATCH, SEQ, D_MODEL),
            y_s.reshape(DEC_BATCH, 1, D_MODEL),
            _token_major(rowst_p, 4),
            _token_major(kvt_s[:KV_ROW].reshape(1, KV_ROW, DEC_BATCH), 4).reshape(
                DEC_BATCH, 1, 4, N_KV_GROUPS, HEAD_DIM),
            _token_major(wint_p[:, :, SEQ - WINDOW:], 2),
            _token_major(win_s, 2),
            jnp.stack(v_p),
            jnp.stack(v_s).reshape(N_A_LAYERS, DEC_BATCH, 1, D_A))
```

```python
import functools

import numpy as np
import jax
import jax.numpy as jnp
from jax import lax
from jax.experimental import pallas as pl
from jax.experimental.pallas import tpu as pltpu

F32 = jnp.float32
BF16 = jnp.bfloat16
HIGHEST = lax.Precision.HIGHEST

D_MODEL = 1024
BATCH = 8
SEQ = 2048
DEPTH = 4
DEC_BATCH = 128
PAST_LEN = 2048
PAGE_SIZE = 128
N_A_LAYERS = DEPTH // 2
CHUNK = 128
D_A = D_MODEL
A_GROUPS = 8
N_HEADS = 16
HEAD_DIM = 64
N_KV_GROUPS = 4
Q_PER_GROUP = 4
CMP_LEN = 32
CMP_STRIDE = 16
SLC_BLOCK = 64
N_TOP = 16
WINDOW = 512
D_FF = 2816
N_EXPERTS = 8
TOP_K = 2
RMS_EPS = 1e-6
LN_EPS = 1e-5
NEG_INF = -1e30
FORCE_SCORE = 1e4

LANES = 128
TM = 512
N_P = BATCH * SEQ
N_PT = N_P // TM
N_TOT = N_P + TM
N_TILES = N_TOT // TM
N_REAL = N_P + DEC_BATCH
TQ = 128
BK_SEL = 256
BK_WIN = 128
N_WIN_KEYS = WINDOW + TQ
N_CMP_PAD = 128
N_SEL_P = SEQ // SLC_BLOCK
N_PAGES = PAST_LEN // PAGE_SIZE
NB_S = 2
NB_C = 2
GD = N_KV_GROUPS * HEAD_DIM
KV_ROW = 4 * GD
KV_WIN = 2 * GD
T_MOE = 512
N_ASSIGN = N_REAL * TOP_K
N_MOE_BLOCKS = -(-N_ASSIGN // T_MOE) + N_EXPERTS
N_MOE_ROWS = N_MOE_BLOCKS * T_MOE
FF_SPLIT = 2
VMEM_LIMIT = 56 * 1024 * 1024

_SLOPES = [float(v) for v in
           (2.0 ** (-8.0 * np.arange(1, N_HEADS + 1, dtype=np.float32) / N_HEADS)).astype(np.float32)]

K_AUG = 2 * HEAD_DIM
AUG_ROWS = 16
AUG_ONE, AUG_HI, AUG_LO, AUG_SEL = 0, 3, 6, 9
POS_LO = 128


def _split3_const(v):
    out = []
    r = np.float32(v)
    for _ in range(3):
        p = np.float32(np.asarray(r, dtype=jnp.bfloat16))
        out.append(float(p))
        r = np.float32(r - p)
    return out


LOG2E = float(np.log2(np.e))
_SLOPES2 = [float(np.float32(np.float32(v) * np.float32(LOG2E))) for v in _SLOPES]
_SLOPE_SPLIT = [_split3_const(v) for v in _SLOPES2]


def _cparams(n_axes):
    return pltpu.CompilerParams(dimension_semantics=("arbitrary",) * n_axes,
                                vmem_limit_bytes=VMEM_LIMIT)


def _const(shape):
    nd = len(shape)
    return pl.BlockSpec(shape, lambda *_: (0,) * nd, pipeline_mode=pl.Buffered(1))


def _rms(x, g):
    return x * lax.rsqrt(jnp.mean(x * x, axis=-1, keepdims=True) + RMS_EPS) * g


def _dot(a, b, **kw):
    return jnp.dot(a, b, preferred_element_type=F32, **kw)


def _dot_nt(a, b, **kw):
    return lax.dot_general(a, b, (((1,), (1,)), ((), ())), preferred_element_type=F32, **kw)


def _a_mixer_kernel(x_ref, nrm_ref, win_ref, lng_ref, lnb_ref, wmix_ref, bias_ref, wout_ref,
                    xo_ref, vp_ref, vs_ref, mixed_ref):
    i = pl.program_id(0)
    x = x_ref[...]
    h = _rms(x, nrm_ref[...]).astype(BF16)
    z = jax.nn.gelu(_dot(h, win_ref[...]))
    u = z[:, :D_A]
    v = z[:, D_A:]
    mu = jnp.mean(v, axis=-1, keepdims=True)
    var = jnp.mean(jnp.square(v - mu), axis=-1, keepdims=True)
    v = (v - mu) * lax.rsqrt(var + LN_EPS) * lng_ref[...] + lnb_ref[...]

    @pl.when((i < N_PT) & (i % (SEQ // TM) == SEQ // TM - 1))
    def _():
        vp_ref[0] = v[TM - CHUNK:, :]

    @pl.when(i == N_PT)
    def _():
        vs_ref[...] = v[:DEC_BATCH, :]

    vb = v.astype(BF16)
    for c in range(TM // CHUNK):
        for g in range(A_GROUPS):
            cols = slice(g * LANES, (g + 1) * LANES)
            rows = slice(c * CHUNK, (c + 1) * CHUNK)
            mixed_ref[rows, cols] = _dot(wmix_ref[0, g], vb[rows, cols]) + bias_ref[0, :, cols]
    t = (u * mixed_ref[...]).astype(BF16)
    xo_ref[...] = x + _dot(t, wout_ref[...])


def _a_mixer(x, nrm, w_in, ln_g, ln_b, wmix, bias, w_out):
    return pl.pallas_call(
        _a_mixer_kernel,
        grid=(N_TILES,),
        in_specs=[
            pl.BlockSpec((TM, D_MODEL), lambda i: (i, 0)),
            _const((1, D_MODEL)),
            _const((D_MODEL, 2 * D_A)),
            _const((1, D_A)),
            _const((1, D_A)),
            pl.BlockSpec((1, A_GROUPS, CHUNK, CHUNK), lambda i: (i // N_PT, 0, 0, 0)),
            pl.BlockSpec((1, CHUNK, D_A), lambda i: (i // N_PT, 0, 0)),
            _const((D_A, D_MODEL)),
        ],
        out_specs=[
            pl.BlockSpec((TM, D_MODEL), lambda i: (i, 0)),
            pl.BlockSpec((1, CHUNK, D_A), lambda i: (jnp.minimum(i // (SEQ // TM), BATCH - 1), 0, 0)),
            pl.BlockSpec((DEC_BATCH, D_A), lambda i: (0, 0)),
        ],
        out_shape=[
            jax.ShapeDtypeStruct((N_TOT, D_MODEL), F32),
            jax.ShapeDtypeStruct((BATCH, CHUNK, D_A), F32),
            jax.ShapeDtypeStruct((DEC_BATCH, D_A), F32),
        ],
        scratch_shapes=[pltpu.VMEM((TM, D_A), F32)],
        compiler_params=_cparams(1),
    )(x, nrm, w_in, ln_g, ln_b, wmix, bias, w_out)


def _swiglu_block(h, wg_ref, wu_ref, wd_ref, lead):
    ffh = D_FF // FF_SPLIT
    out = None
    for s in range(FF_SPLIT):
        cols = slice(s * ffh, (s + 1) * ffh)
        g = _dot(h, wg_ref[lead + (slice(None), cols)])
        u = _dot(h, wu_ref[lead + (slice(None), cols)])
        a = (jax.nn.silu(g) * u).astype(BF16)
        part = _dot(a, wd_ref[lead + (cols, slice(None))])
        out = part if out is None else out + part
    return out


def _ffn_kernel(x_ref, nrm_ref, wg_ref, wu_ref, wd_ref, o_ref):
    x = x_ref[...]
    h = _rms(x, nrm_ref[...]).astype(BF16)
    o_ref[...] = x + _swiglu_block(h, wg_ref, wu_ref, wd_ref, ())


def _ffn(x, nrm, wg, wu, wd):
    return pl.pallas_call(
        _ffn_kernel,
        grid=(N_TILES,),
        in_specs=[
            pl.BlockSpec((TM, D_MODEL), lambda i: (i, 0)),
            _const((1, D_MODEL)),
            _const((D_MODEL, D_FF)),
            _const((D_MODEL, D_FF)),
            _const((D_FF, D_MODEL)),
        ],
        out_specs=pl.BlockSpec((TM, D_MODEL), lambda i: (i, 0)),
        out_shape=jax.ShapeDtypeStruct((N_TOT, D_MODEL), F32),
        compiler_params=_cparams(1),
    )(x, nrm, wg, wu, wd)


def _router_kernel(x_ref, nrm_ref, wr_ref, br_ref, h_ref, r_ref):
    h = _rms(x_ref[...], nrm_ref[...])
    h_ref[...] = h
    logits = _dot(h, wr_ref[...], precision=HIGHEST) + br_ref[...]
    lane = lax.broadcasted_iota(jnp.int32, logits.shape, 1).astype(F32)
    big = float(LANES)
    m1 = jnp.max(logits, axis=1, keepdims=True)
    i1 = jnp.min(jnp.where(logits == m1, lane, big), axis=1, keepdims=True)
    l2 = jnp.where(lane == i1, -jnp.inf, logits)
    m2 = jnp.max(l2, axis=1, keepdims=True)
    i2 = jnp.min(jnp.where(l2 == m2, lane, big), axis=1, keepdims=True)
    e = jnp.exp(m2 - m1)
    g1 = 1.0 / (1.0 + e)
    g2 = e / (1.0 + e)
    r_ref[...] = jnp.where(lane == 0.0, i1, jnp.where(lane == 1.0, i2,
                           jnp.where(lane == 2.0, g1, jnp.where(lane == 3.0, g2, 0.0))))


def _router(x, nrm, wr, br):
    return pl.pallas_call(
        _router_kernel,
        grid=(N_TILES,),
        in_specs=[
            pl.BlockSpec((TM, D_MODEL), lambda i: (i, 0)),
            _const((1, D_MODEL)),
            _const((D_MODEL, LANES)),
            _const((1, LANES)),
        ],
        out_specs=[
            pl.BlockSpec((TM, D_MODEL), lambda i: (i, 0)),
            pl.BlockSpec((TM, LANES), lambda i: (i, 0)),
        ],
        out_shape=[
            jax.ShapeDtypeStruct((N_TOT, D_MODEL), F32),
            jax.ShapeDtypeStruct((N_TOT, LANES), F32),
        ],
        compiler_params=_cparams(1),
    )(x, nrm, wr, br)


def _moe_kernel(be_ref, na_ref, rt_ref, h_hbm, wg_ref, wu_ref, wd_ref, o_ref, xbuf, sem):
    i = pl.program_id(0)
    na = na_ref[0]
    slot = i % 2

    def start_gather(blk, s):
        base = blk * T_MOE

        def body(r, carry):
            pltpu.make_async_copy(h_hbm.at[rt_ref[base + r]], xbuf.at[s, r], sem.at[s]).start()
            return carry

        lax.fori_loop(0, T_MOE, body, 0, unroll=8)

    @pl.when(i == 0)
    def _():
        start_gather(0, 0)

    @pl.when(i + 1 < na)
    def _():
        start_gather(i + 1, 1 - slot)

    @pl.when(i < na)
    def _():
        pltpu.make_async_copy(h_hbm.at[pl.ds(0, T_MOE)], xbuf.at[slot], sem.at[slot]).wait()
        x = jnp.concatenate([xbuf[slot, :, c, :] for c in range(D_MODEL // LANES)], axis=1).astype(BF16)
        o_ref[...] = _swiglu_block(x, wg_ref, wu_ref, wd_ref, (0, 0))

    @pl.when(i >= na)
    def _():
        o_ref[...] = jnp.zeros(o_ref.shape, o_ref.dtype)


def _moe_experts(blk_expert, n_active, row_token, h_tiles, wg, wu, wd, layer):
    return pl.pallas_call(
        _moe_kernel,
        grid_spec=pltpu.PrefetchScalarGridSpec(
            num_scalar_prefetch=3,
            grid=(N_MOE_BLOCKS,),
            in_specs=[
                pl.BlockSpec(memory_space=pl.ANY),
                pl.BlockSpec((1, 1, D_MODEL, D_FF), lambda i, be, na, rt: (layer, be[i], 0, 0)),
                pl.BlockSpec((1, 1, D_MODEL, D_FF), lambda i, be, na, rt: (layer, be[i], 0, 0)),
                pl.BlockSpec((1, 1, D_FF, D_MODEL), lambda i, be, na, rt: (layer, be[i], 0, 0)),
            ],
            out_specs=pl.BlockSpec((T_MOE, D_MODEL), lambda i, be, na, rt: (i, 0)),
            scratch_shapes=[
                pltpu.VMEM((2, T_MOE, D_MODEL // LANES, LANES), F32),
                pltpu.SemaphoreType.DMA((2,)),
            ],
        ),
        out_shape=jax.ShapeDtypeStruct((N_MOE_ROWS, D_MODEL), F32),
        compiler_params=_cparams(1),
    )(blk_expert, n_active, row_token, h_tiles, wg, wu, wd)


def _moe_layer(x, nrm, wr, br, wg, wu, wd, layer):
    h, r = _router(x, nrm, wr, br)
    r = r[:N_REAL]
    expert = r[:, 0:2].astype(jnp.int32).reshape(N_ASSIGN)
    gate = r[:, 2:4]
    onehot = (expert[:, None] == jnp.arange(N_EXPERTS, dtype=jnp.int32)[None, :]).astype(jnp.int32)
    csum = jnp.cumsum(onehot, axis=0)
    counts = csum[-1]
    rank = jnp.sum(csum * onehot, axis=1) - 1
    padded = (counts + T_MOE - 1) // T_MOE * T_MOE
    pad_end = jnp.cumsum(padded)
    pad_start = pad_end - padded
    dest = jnp.sum(pad_start[None, :] * onehot, axis=1) + rank
    blk_start = jnp.arange(N_MOE_BLOCKS, dtype=jnp.int32) * T_MOE
    blk_expert = jnp.minimum(jnp.sum(pad_end[None, :] <= blk_start[:, None], axis=1),
                             N_EXPERTS - 1).astype(jnp.int32)
    n_active = (pad_end[-1:] // T_MOE).astype(jnp.int32)
    n_pad = N_MOE_ROWS - N_ASSIGN
    gaps = padded - counts
    gap_end = jnp.cumsum(gaps)
    gap_start = gap_end - gaps
    p = jnp.arange(n_pad, dtype=jnp.int32)
    e_p = jnp.sum(gap_end[None, :] <= p[:, None], axis=1)
    hot_p = (e_p[:, None] == jnp.arange(N_EXPERTS, dtype=jnp.int32)[None, :]).astype(jnp.int32)
    row_in = jnp.sum(hot_p * (pad_start + counts - gap_start)[None, :], axis=1) + p
    pad_rows = jnp.where(e_p < N_EXPERTS, row_in, pad_end[-1] + p - gap_end[-1])
    keys = jnp.concatenate([dest, pad_rows]).astype(jnp.int32)
    vals = jnp.concatenate([jnp.arange(N_ASSIGN, dtype=jnp.int32) // TOP_K, jnp.full((n_pad,), N_REAL, jnp.int32)])
    _, row_token = lax.sort_key_val(keys, vals)
    h_tiles = h.reshape(N_TOT, D_MODEL // LANES, LANES)
    y_rows = _moe_experts(blk_expert, n_active, row_token, h_tiles, wg, wu, wd, layer)
    d2 = dest.reshape(N_REAL, TOP_K)
    y = gate[:, 0:1] * y_rows[d2[:, 0]] + gate[:, 1:2] * y_rows[d2[:, 1]]
    return x.at[:N_REAL].add(y)


def _kvproj_kernel(x_ref, nrm_ref, wkv_ref, wkvt_ref, rowst_ref, wint_ref, kvt_s_ref, kvn_s_ref, cmp_ref,
                   ksel_ref, kwin_ref, vselt_ref, vwint_ref):
    i = pl.program_id(0)
    h = _rms(x_ref[...], nrm_ref[...]).astype(BF16)
    kv = _dot(h, wkv_ref[...])
    kvt = _dot_nt(wkvt_ref[...], h)

    @pl.when(i < N_PT)
    def _():
        rowst_ref[0] = kvt[:KV_ROW, :]
        wint_ref[0] = kvt[KV_ROW:, :]
        for cb in range(2 * GD // LANES):
            cmp_ref[cb] = kv[:, cb * LANES:(cb + 1) * LANES]

    @pl.when(i == N_PT)
    def _():
        kvt_s_ref[...] = kvt[:, :DEC_BATCH]
        kvn_s_ref[...] = kv[:DEC_BATCH, :]

    kvt_b = kvt.astype(BF16)
    pos = (i % (SEQ // TM)) * TM + lax.broadcasted_iota(jnp.int32, (TM, HEAD_DIM), 0)
    a_col = lax.broadcasted_iota(jnp.int32, (TM, HEAD_DIM), 1)
    hi = (pos // POS_LO * POS_LO).astype(F32)
    lo = (pos % POS_LO).astype(F32)
    aug = jnp.where(a_col < AUG_HI, 1.0, jnp.where(a_col < AUG_LO, hi, jnp.where(a_col < AUG_SEL, lo, 0.0)))
    sel_hot = jnp.where((a_col >= AUG_SEL) & (a_col - AUG_SEL == (pos // SLC_BLOCK) % (BK_SEL // SLC_BLOCK)),
                        1.0, 0.0)
    for g in range(N_KV_GROUPS):
        k_s = kv[:, 2 * GD + g * HEAD_DIM:2 * GD + (g + 1) * HEAD_DIM]
        k_w = kv[:, 4 * GD + g * HEAD_DIM:4 * GD + (g + 1) * HEAD_DIM]
        ksel_ref[g] = jnp.concatenate([k_s, aug + sel_hot], axis=1).astype(BF16)
        kwin_ref[g] = jnp.concatenate([k_w, aug], axis=1).astype(BF16)
        for j in range(TM // BK_SEL):
            vselt_ref[g, j] = kvt_b[3 * GD + g * HEAD_DIM:3 * GD + (g + 1) * HEAD_DIM,
                                    j * BK_SEL:(j + 1) * BK_SEL]
        for j in range(TM // BK_WIN):
            vwint_ref[g, j] = kvt_b[5 * GD + g * HEAD_DIM:5 * GD + (g + 1) * HEAD_DIM,
                                    j * BK_WIN:(j + 1) * BK_WIN]


def _kvproj(x, nrm, wkv, wkvt):
    n_kv = KV_ROW + KV_WIN
    tiles_per_seq = SEQ // TM
    ip = lambda i: jnp.minimum(i, N_PT - 1)
    return pl.pallas_call(
        _kvproj_kernel,
        grid=(N_TILES,),
        in_specs=[
            pl.BlockSpec((TM, D_MODEL), lambda i: (i, 0)),
            _const((1, D_MODEL)),
            _const((D_MODEL, n_kv)),
            _const((n_kv, D_MODEL)),
        ],
        out_specs=[
            pl.BlockSpec((1, KV_ROW, TM), lambda i: (ip(i) // tiles_per_seq, 0, ip(i) % tiles_per_seq)),
            pl.BlockSpec((1, KV_WIN, TM), lambda i: (ip(i) // tiles_per_seq, 0, ip(i) % tiles_per_seq)),
            pl.BlockSpec((n_kv, DEC_BATCH), lambda i: (0, 0)),
            pl.BlockSpec((DEC_BATCH, n_kv), lambda i: (0, 0)),
            pl.BlockSpec((2 * GD // LANES, TM, LANES), lambda i: (0, ip(i), 0)),
            pl.BlockSpec((N_KV_GROUPS, TM, K_AUG), lambda i: (0, i, 0)),
            pl.BlockSpec((N_KV_GROUPS, TM, K_AUG), lambda i: (0, i, 0)),
            pl.BlockSpec((N_KV_GROUPS, TM // BK_SEL, HEAD_DIM, BK_SEL), lambda i: (0, i, 0, 0)),
            pl.BlockSpec((N_KV_GROUPS, TM // BK_WIN, HEAD_DIM, BK_WIN), lambda i: (0, i, 0, 0)),
        ],
        out_shape=[
            jax.ShapeDtypeStruct((BATCH, KV_ROW, SEQ), F32),
            jax.ShapeDtypeStruct((BATCH, KV_WIN, SEQ), F32),
            jax.ShapeDtypeStruct((n_kv, DEC_BATCH), F32),
            jax.ShapeDtypeStruct((DEC_BATCH, n_kv), F32),
            jax.ShapeDtypeStruct((2 * GD // LANES, N_P, LANES), F32),
            jax.ShapeDtypeStruct((N_KV_GROUPS, N_TOT, K_AUG), BF16),
            jax.ShapeDtypeStruct((N_KV_GROUPS, N_TOT, K_AUG), BF16),
            jax.ShapeDtypeStruct((N_KV_GROUPS, N_TOT // BK_SEL, HEAD_DIM, BK_SEL), BF16),
            jax.ShapeDtypeStruct((N_KV_GROUPS, N_TOT // BK_WIN, HEAD_DIM, BK_WIN), BF16),
        ],
        compiler_params=_cparams(1),
    )(x, nrm, wkv, wkvt)


def _compress_body(nat_ref, leads, ws_ref, c_ref, w2_ref, kc_ref, vct_ref, prepare=None):
    n_seg = SEQ // CMP_STRIDE
    for kind in range(2):
        if prepare is not None:
            prepare(kind)
        y = None
        for r in range(CMP_STRIDE):
            xr = jnp.concatenate([jnp.concatenate(
                [nat_ref[lead + (kind * (GD // LANES) + j, pl.ds(r, n_seg, stride=CMP_STRIDE), slice(None))]
                 for j in range(GD // LANES)], axis=1) for lead in leads], axis=0)
            part = _dot(xr.astype(BF16), ws_ref[kind, r * GD:(r + 1) * GD, :])
            y = part if y is None else y + part
        a = y[:, :GD]
        b = pltpu.roll(y[:, GD:], len(leads) * N_CMP_PAD - 1, axis=0)
        hid = jax.nn.gelu(a + b + c_ref[kind])
        out = _dot(hid.astype(BF16), w2_ref[kind])
        for i in range(len(leads)):
            blocks = out[i * N_CMP_PAD:(i + 1) * N_CMP_PAD, :]
            if kind == 0:
                kc_ref[i] = blocks.astype(BF16)
            else:
                vct_ref[i] = blocks.T.astype(BF16)


def _compress_prompt_kernel(cmp_ref, ws_ref, c_ref, w2_ref, kc_ref, vct_ref):
    _compress_body(cmp_ref, [()], ws_ref, c_ref, w2_ref, kc_ref, vct_ref)


def _compress_sample_kernel(pt_ref, *refs):
    n_pg = NB_C * N_PAGES
    pages = [refs[j * N_PAGES:(j + 1) * N_PAGES] for j in range(NB_C)]
    ws_ref, c_ref, w2_ref, kc_ref, vct_ref, nat_ref = refs[n_pg:]

    def to_token_major(kind):
        for j in range(NB_C):
            for p, page in enumerate(pages[j]):
                for cb in range(kind * (GD // LANES), (kind + 1) * (GD // LANES)):
                    nat_ref[j, cb, p * PAGE_SIZE:(p + 1) * PAGE_SIZE, :] = page[0, cb * LANES:(cb + 1) * LANES, :].T

    _compress_body(nat_ref, [(j,) for j in range(NB_C)], ws_ref, c_ref, w2_ref, kc_ref, vct_ref,
                   prepare=to_token_major)


_CMP_W_SHAPES = [(2, CMP_STRIDE * GD, 2 * GD), (2, 1, GD), (2, GD, GD)]


def _compress_prompt(cmp_nat, ws, cvec, w2bd):
    return pl.pallas_call(
        _compress_prompt_kernel,
        grid=(BATCH,),
        in_specs=[pl.BlockSpec((2 * GD // LANES, SEQ, LANES), lambda b: (0, b, 0))]
        + [_const(s) for s in _CMP_W_SHAPES],
        out_specs=[
            pl.BlockSpec((1, N_CMP_PAD, GD), lambda b: (b, 0, 0)),
            pl.BlockSpec((1, GD, N_CMP_PAD), lambda b: (b, 0, 0)),
        ],
        out_shape=[
            jax.ShapeDtypeStruct((BATCH, N_CMP_PAD, GD), BF16),
            jax.ShapeDtypeStruct((BATCH, GD, N_CMP_PAD), BF16),
        ],
        compiler_params=_cparams(1),
    )(cmp_nat, ws, cvec, w2bd)


def _page_spec(n_tok, j, p, half):
    return pl.BlockSpec((1, 2 * GD, PAGE_SIZE), lambda b, pt: (pt[b * n_tok + j, p], half, 0))


def _const_sp(shape):
    return pl.BlockSpec(shape, functools.partial(lambda nd, b, pt: (0,) * nd, len(shape)),
                        pipeline_mode=pl.Buffered(1))


def _compress_sample(page_table, cache_t, ws, cvec, w2bd):
    return pl.pallas_call(
        _compress_sample_kernel,
        grid_spec=pltpu.PrefetchScalarGridSpec(
            num_scalar_prefetch=1,
            grid=(DEC_BATCH // NB_C,),
            in_specs=[_page_spec(NB_C, j, p, 0) for j in range(NB_C) for p in range(N_PAGES)]
            + [_const_sp(s) for s in _CMP_W_SHAPES],
            out_specs=[
                pl.BlockSpec((NB_C, N_CMP_PAD, GD), lambda b, pt: (b, 0, 0)),
                pl.BlockSpec((NB_C, GD, N_CMP_PAD), lambda b, pt: (b, 0, 0)),
            ],
            scratch_shapes=[pltpu.VMEM((NB_C, 2 * GD // LANES, PAST_LEN, LANES), F32)],
        ),
        out_shape=[
            jax.ShapeDtypeStruct((DEC_BATCH, N_CMP_PAD, GD), BF16),
            jax.ShapeDtypeStruct((DEC_BATCH, GD, N_CMP_PAD), BF16),
        ],
        compiler_params=_cparams(1),
    )(page_table, *([cache_t] * (NB_C * N_PAGES)), ws, cvec, w2bd)


def _qproj_kernel(x_ref, nrm_ref, w_ref, bg_ref, q_ref, g_ref):
    h = _rms(x_ref[...], nrm_ref[...]).astype(BF16)
    p = _dot(h, w_ref[...])
    q_ref[...] = p[:, :D_MODEL] * (HEAD_DIM ** -0.5)
    g_ref[...] = jax.nn.sigmoid(p[:, D_MODEL:] + bg_ref[...])


def _qproj(x, nrm, w, bg):
    return pl.pallas_call(
        _qproj_kernel,
        grid=(N_TILES,),
        in_specs=[
            pl.BlockSpec((TM, D_MODEL), lambda i: (i, 0)),
            _const((1, D_MODEL)),
            _const((D_MODEL, D_MODEL + LANES)),
            _const((1, LANES)),
        ],
        out_specs=[
            pl.BlockSpec((TM, D_MODEL), lambda i: (i, 0)),
            pl.BlockSpec((TM, LANES), lambda i: (i, 0)),
        ],
        out_shape=[
            jax.ShapeDtypeStruct((N_TOT, D_MODEL), F32),
            jax.ShapeDtypeStruct((N_TOT, LANES), F32),
        ],
        compiler_params=_cparams(1),
    )(x, nrm, w, bg)


def _top_mask(score, n_sel, idx, axis):
    cnt = jnp.zeros(score.shape, F32)
    for i in range(n_sel):
        row = score[i:i + 1, :] if axis == 0 else score[:, i:i + 1]
        beats = (row > score) | ((row == score) & (idx > i))
        cnt = cnt + jnp.where(beats, 1.0, 0.0)
    return jnp.where(cnt < float(N_TOP), 0.0, NEG_INF)


def _split3(a):
    a1 = a.astype(BF16).astype(F32)
    r1 = a - a1
    a2 = r1.astype(BF16).astype(F32)
    a3 = (r1 - a2).astype(BF16).astype(F32)
    return a1, a2, a3


def _nsa_prompt_kernel(q_ref, gt_ref, x_ref, kc_ref, vct_ref, ks_ref, vst_ref, kw_ref, vwt_ref,
                       ovl_ref, wout_ref, o_ref, qgt_ref, aug_ref, st_ref, ot_ref, selneg_ref, m_ref, l_ref,
                       acc_ref):
    qi = pl.program_id(1)
    t0 = qi * TQ
    qt_all = q_ref[...].T
    gt = gt_ref[...].T
    tq = t0 + lax.broadcasted_iota(jnp.int32, (1, TQ), 1)
    tq_f = tq.astype(F32)
    groups = range(N_KV_GROUPS)
    heads = range(Q_PER_GROUP)
    lanes = [slice(r * TQ, (r + 1) * TQ) for r in heads]
    wide = Q_PER_GROUP * TQ

    n_idx = lax.broadcasted_iota(jnp.int32, (N_CMP_PAD, 1), 0)
    d_c = tq - (n_idx * CMP_STRIDE + CMP_LEN - 1)
    ok_c = d_c >= 0
    d_cf = d_c.astype(F32)
    j_idx = lax.broadcasted_iota(jnp.int32, (N_SEL_P, 1), 0)
    cur = tq // SLC_BLOCK
    valid = j_idx <= cur
    forced = (j_idx == 0) | (j_idx == cur) | (j_idx == cur - 1)

    rid = lax.broadcasted_iota(jnp.int32, (AUG_ROWS, TQ), 0)
    qc_t = []
    for g in groups:
        blocks = []
        for r in heads:
            hd = g * Q_PER_GROUP + r
            s1, s2, s3 = _SLOPE_SPLIT[hd]
            a1, a2, a3 = _split3(-_SLOPES2[hd] * tq_f)
            slope_rows = jnp.where((rid == AUG_HI) | (rid == AUG_LO), s1,
                                   jnp.where((rid == AUG_HI + 1) | (rid == AUG_LO + 1), s2,
                                             jnp.where((rid == AUG_HI + 2) | (rid == AUG_LO + 2), s3, 0.0)))
            blocks.append(jnp.where(rid == AUG_ONE, a1, jnp.where(rid == AUG_ONE + 1, a2,
                                    jnp.where(rid == AUG_ONE + 2, a3, slope_rows))))
        aug = jnp.concatenate(blocks, axis=1)
        aug_ref[g] = aug
        qg = jnp.concatenate(
            [qt_all[(g * Q_PER_GROUP + r) * HEAD_DIM:(g * Q_PER_GROUP + r + 1) * HEAD_DIM, :]
             for r in heads], axis=1)
        qc_t.append(qg.astype(BF16))
        qgt_ref[g, 0:HEAD_DIM, :] = (qg * LOG2E).astype(BF16)
        qgt_ref[g, HEAD_DIM:HEAD_DIM + AUG_ROWS, :] = aug.astype(BF16)
        qgt_ref[g, HEAD_DIM + AUG_ROWS:, :] = jnp.zeros((K_AUG - HEAD_DIM - AUG_ROWS, wide), BF16)

    oc_t = []
    sts_c = [_dot(kc_ref[0, :, g * HEAD_DIM:(g + 1) * HEAD_DIM], qc_t[g]) for g in groups]
    for g in groups:
        st = sts_c[g]
        psum = jnp.zeros((N_CMP_PAD, TQ), F32)
        ps = []
        for r in heads:
            s = st[:, lanes[r]] - _SLOPES[g * Q_PER_GROUP + r] * d_cf
            s = jnp.where(ok_c, s, NEG_INF)
            p = jnp.exp(s - jnp.max(s, axis=0, keepdims=True))
            p = p / jnp.sum(p, axis=0, keepdims=True)
            p = jnp.where(ok_c, p, 0.0)
            psum = psum + p
            ps.append(p.astype(BF16))
        oc_t.append(_dot(vct_ref[0, g * HEAD_DIM:(g + 1) * HEAD_DIM, :], jnp.concatenate(ps, axis=1)))
        imp = _dot(ovl_ref[...], psum, precision=HIGHEST)
        score = jnp.where(valid, jnp.where(forced, FORCE_SCORE, imp), NEG_INF)
        selneg_ref[g] = _top_mask(score, N_SEL_P, j_idx, 0)

    m_ref[...] = jnp.full(m_ref.shape, NEG_INF, F32)
    l_ref[...] = jnp.zeros(l_ref.shape, F32)
    acc_ref[...] = jnp.zeros(acc_ref.shape, F32)
    blocks_per_chunk = BK_SEL // SLC_BLOCK
    rid_w = lax.broadcasted_iota(jnp.int32, (AUG_ROWS, wide), 0)

    def score_group(c, g):
        k_rows = pl.ds(pl.multiple_of(c * BK_SEL, BK_SEL), BK_SEL)
        blk = aug_ref[g]
        for jj in range(blocks_per_chunk):
            row = selneg_ref[g, pl.ds(c * blocks_per_chunk + jj, 1), :]
            blk = jnp.where(rid_w == AUG_SEL + jj, jnp.concatenate([row] * Q_PER_GROUP, axis=1), blk)
        qgt_ref[g, HEAD_DIM:HEAD_DIM + AUG_ROWS, :] = blk.astype(BF16)
        st_ref[g] = _dot(ks_ref[g, k_rows, :], qgt_ref[g])

    def sel_chunk(c, diagonal):
        if diagonal:
            causal = (lax.broadcasted_iota(jnp.int32, (BK_SEL, TQ), 1)
                      - lax.broadcasted_iota(jnp.int32, (BK_SEL, TQ), 0) + (t0 - c * BK_SEL)) >= 0
        m_all = m_ref[...]
        l_all = l_ref[...]
        pvs, alpha_all, m_out, l_out = [], [], [], []
        for g in groups:
            ps, alphas, ms, ls = [], [], [], []
            for r in heads:
                s = st_ref[g, :, lanes[r]]
                if diagonal:
                    s = jnp.where(causal, s, NEG_INF)
                m_old = m_all[g, :, lanes[r]]
                m_new = jnp.maximum(m_old, jnp.max(s, axis=0, keepdims=True))
                alpha = jnp.exp2(m_old - m_new)
                p = jnp.exp2(s - m_new)
                ls.append(alpha * l_all[g, :, lanes[r]] + jnp.sum(p, axis=0, keepdims=True))
                ms.append(m_new)
                ps.append(p.astype(BF16))
                alphas.append(alpha)
            pvs.append(_dot(vst_ref[g, c], jnp.concatenate(ps, axis=1)))
            if not diagonal:
                score_group(c + 1, g)
            alpha_all.append(jnp.concatenate(alphas, axis=1))
            m_out.append(jnp.concatenate(ms, axis=1))
            l_out.append(jnp.concatenate(ls, axis=1))
        for g in groups:
            m_ref[g] = m_out[g]
            l_ref[g] = l_out[g]
            acc_ref[g] = acc_ref[g] * alpha_all[g] + pvs[g]

    def sel_body(c, carry):
        sel_chunk(c, False)
        return carry

    c_last = (t0 + TQ - 1) // BK_SEL
    for g in groups:
        score_group(0, g)
    lax.fori_loop(0, c_last, sel_body, 0)
    sel_chunk(c_last, True)

    for g in groups:
        qgt_ref[g, HEAD_DIM:HEAD_DIM + AUG_ROWS, :] = aug_ref[g].astype(BF16)
    k_start = pl.multiple_of(jnp.maximum(t0 - WINDOW, 0), BK_WIN)
    c_start = k_start // BK_WIN
    d_w = (lax.broadcasted_iota(jnp.int32, (N_WIN_KEYS, TQ), 1)
           - lax.broadcasted_iota(jnp.int32, (N_WIN_KEYS, TQ), 0)) + (t0 - k_start)
    madd_w = jnp.where((d_w >= 0) & (d_w < WINDOW), 0.0, NEG_INF)

    sts_w = [_dot(kw_ref[g, pl.ds(k_start, N_WIN_KEYS), :], qgt_ref[g]) for g in groups]
    for g in groups:
        st = sts_w[g]
        ps = []
        ls = []
        for r in heads:
            s = st[:, lanes[r]] + madd_w
            p = jnp.exp2(s - jnp.max(s, axis=0, keepdims=True))
            ls.append(jnp.sum(p, axis=0, keepdims=True))
            ps.append(p.astype(BF16))
        pt = jnp.concatenate(ps, axis=1)
        ow_t = None
        for c in range(N_WIN_KEYS // BK_WIN):
            part = _dot(vwt_ref[g, c_start + c], pt[c * BK_WIN:(c + 1) * BK_WIN, :])
            ow_t = part if ow_t is None else ow_t + part
        ow_t = ow_t / jnp.concatenate(ls, axis=1)
        os_t = acc_ref[g] / l_ref[g]
        for r in heads:
            hd = g * Q_PER_GROUP + r
            ot_ref[hd * HEAD_DIM:(hd + 1) * HEAD_DIM, :] = (
                gt[3 * hd:3 * hd + 1, :] * oc_t[g][:, lanes[r]]
                + gt[3 * hd + 1:3 * hd + 2, :] * os_t[:, lanes[r]]
                + gt[3 * hd + 2:3 * hd + 3, :] * ow_t[:, lanes[r]])

    o = ot_ref[...].T.astype(BF16)
    o_ref[...] = x_ref[...] + _dot(o, wout_ref[...])


def _nsa_prompt(q, gates, x, kc, vct, ksel, vselt, kwin, vwint, ovl_t, w_out):
    nq = SEQ // TQ
    tile = lambda b, qi: (b * nq + qi, 0)
    wide = Q_PER_GROUP * TQ
    return pl.pallas_call(
        _nsa_prompt_kernel,
        grid=(BATCH, nq),
        in_specs=[
            pl.BlockSpec((TQ, D_MODEL), tile),
            pl.BlockSpec((TQ, LANES), tile),
            pl.BlockSpec((TQ, D_MODEL), tile),
            pl.BlockSpec((1, N_CMP_PAD, GD), lambda b, qi: (b, 0, 0)),
            pl.BlockSpec((1, GD, N_CMP_PAD), lambda b, qi: (b, 0, 0)),
            pl.BlockSpec((N_KV_GROUPS, SEQ, K_AUG), lambda b, qi: (0, b, 0)),
            pl.BlockSpec((N_KV_GROUPS, SEQ // BK_SEL, HEAD_DIM, BK_SEL), lambda b, qi: (0, b, 0, 0)),
            pl.BlockSpec((N_KV_GROUPS, SEQ, K_AUG), lambda b, qi: (0, b, 0)),
            pl.BlockSpec((N_KV_GROUPS, SEQ // BK_WIN, HEAD_DIM, BK_WIN), lambda b, qi: (0, b, 0, 0)),
            _const((N_SEL_P, N_CMP_PAD)),
            _const((D_MODEL, D_MODEL)),
        ],
        out_specs=pl.BlockSpec((TQ, D_MODEL), tile),
        out_shape=jax.ShapeDtypeStruct((N_TOT, D_MODEL), F32),
        input_output_aliases={2: 0},
        scratch_shapes=[
            pltpu.VMEM((N_KV_GROUPS, K_AUG, wide), BF16),
            pltpu.VMEM((N_KV_GROUPS, AUG_ROWS, wide), F32),
            pltpu.VMEM((N_KV_GROUPS, BK_SEL, wide), F32),
            pltpu.VMEM((D_MODEL, TQ), F32),
            pltpu.VMEM((N_KV_GROUPS, N_SEL_P, TQ), F32),
            pltpu.VMEM((N_KV_GROUPS, 1, wide), F32),
            pltpu.VMEM((N_KV_GROUPS, 1, wide), F32),
            pltpu.VMEM((N_KV_GROUPS, HEAD_DIM, wide), F32),
        ],
        compiler_params=_cparams(2),
    )(q, gates, x, kc, vct, ksel, vselt, kwin, vwint, ovl_t, w_out)


def _nsa_sample_kernel(pt_ref, q_ref, g_ref, kc_ref, vct_ref, kvn_ref, wcol_ref, win_ref, *refs):
    n_pg = NB_S * N_PAGES
    pages = [refs[j * N_PAGES:(j + 1) * N_PAGES] for j in range(NB_S)]
    slope_ref, ovl_ref, exp_ref, o_ref = refs[n_pg:n_pg + 4]
    wino_ref = refs[n_pg + 4] if len(refs) > n_pg + 4 else None
    toks = range(NB_S)
    rows = NB_S * N_HEADS
    b0 = pl.program_id(0) * NB_S
    t = PAST_LEN

    def stack(parts):
        return jnp.concatenate(parts, axis=0)

    def per_head(row):
        return jnp.broadcast_to(row, (N_HEADS, row.shape[1]))

    hrow = lax.broadcasted_iota(jnp.int32, (N_HEADS, GD), 0)
    col = lax.broadcasted_iota(jnp.int32, (N_HEADS, GD), 1)
    own1 = (col // HEAD_DIM) == (hrow // Q_PER_GROUP)
    own = stack([own1] * NB_S)
    qbd = []
    for j in toks:
        q = q_ref[j]
        qh = jnp.concatenate([q[:, h * HEAD_DIM:(h + 1) * HEAD_DIM] for h in range(N_HEADS)], axis=0)
        q4 = jnp.concatenate([qh] * N_KV_GROUPS, axis=1)
        qbd.append(jnp.where(own1, q4, 0.0).astype(BF16))
    qbd_f = stack(qbd).astype(F32)
    slope = stack([slope_ref[:, 0:1]] * NB_S)
    kvn = [kvn_ref[j] for j in toks]

    def new_key_score(lo):
        k_new = stack([per_head(kvn[j][:, lo:lo + GD]) for j in toks])
        return jnp.sum(qbd_f * k_new.astype(BF16).astype(F32), axis=1, keepdims=True)

    def new_val(p_new, lo):
        v_new = stack([per_head(kvn[j][:, lo:lo + GD]) for j in toks])
        return p_new.astype(BF16).astype(F32) * v_new.astype(BF16).astype(F32)

    win = [win_ref[j] for j in toks]
    sc_raw = stack([_dot_nt(qbd[j], kc_ref[j]) for j in toks])
    sw_raw = stack([_dot(qbd[j], win[j][0:GD, :].astype(BF16)) for j in toks])
    s_all = stack([jnp.concatenate([_dot(qbd[j], pg_ref[0, 0:GD, :].astype(BF16)) for pg_ref in pages[j]], axis=1)
                   for j in toks])

    i_w = lax.broadcasted_iota(jnp.int32, (1, WINDOW), 1)
    s_w = sw_raw - slope * (WINDOW - i_w).astype(F32)
    s_w = jnp.where(i_w >= 1, s_w, NEG_INF)
    s_wn = new_key_score(4 * GD)
    m = jnp.maximum(jnp.max(s_w, axis=1, keepdims=True), s_wn)
    p_w = jnp.exp(s_w - m)
    p_wn = jnp.exp(s_wn - m)
    l = jnp.sum(p_w, axis=1, keepdims=True) + p_wn
    p_wb = p_w.astype(BF16)
    pv_w = stack([_dot_nt(p_wb[j * N_HEADS:(j + 1) * N_HEADS, :], win[j][GD:, :].astype(BF16)) for j in toks])
    o_w = (new_val(p_wn, 5 * GD) + pv_w) / l

    if wino_ref is not None:
        lane_b = lax.broadcasted_iota(jnp.int32, wcol_ref.shape, 1)
        lane_w = lax.broadcasted_iota(jnp.int32, win[0].shape, 1)
        for j in toks:
            new_col = jnp.sum(jnp.where(lane_b == b0 + j, wcol_ref[...], 0.0), axis=1, keepdims=True)
            wino_ref[j] = jnp.where(lane_w == WINDOW - 1, new_col, pltpu.roll(win[j], WINDOW - 1, axis=1))

    lane_c = lax.broadcasted_iota(jnp.int32, (1, N_CMP_PAD), 1)
    d_c = t - (lane_c * CMP_STRIDE + CMP_LEN - 1)
    ok_c = d_c >= 0
    s = sc_raw - slope * d_c.astype(F32)
    s = jnp.where(ok_c, s, NEG_INF)
    p = jnp.exp(s - jnp.max(s, axis=1, keepdims=True))
    p = p / jnp.sum(p, axis=1, keepdims=True)
    p = jnp.where(ok_c, p, 0.0)
    p_b = p.astype(BF16)
    o_c = stack([_dot_nt(p_b[j * N_HEADS:(j + 1) * N_HEADS, :], vct_ref[j]) for j in toks])

    grp_r = lax.broadcasted_iota(jnp.int32, (rows, rows), 0) // Q_PER_GROUP
    grp_c = lax.broadcasted_iota(jnp.int32, (rows, rows), 1) // Q_PER_GROUP
    pg = _dot(jnp.where(grp_r == grp_c, 1.0, 0.0), p, precision=HIGHEST)
    imp = _dot(pg, ovl_ref[...], precision=HIGHEST)
    j_idx = lax.broadcasted_iota(jnp.int32, (1, LANES), 1)
    cur = t // SLC_BLOCK
    forced = (j_idx == 0) | (j_idx == cur) | (j_idx == cur - 1)
    score = jnp.where(j_idx <= cur, jnp.where(forced, FORCE_SCORE, imp), NEG_INF)
    selneg = _top_mask(score, cur + 1, j_idx, 1)
    sel = jnp.where(selneg == 0.0, 1.0, 0.0).astype(BF16)
    selexp = _dot(sel, exp_ref[...])

    pos = lax.broadcasted_iota(jnp.int32, (1, PAST_LEN), 1)
    s_all = jnp.where(selexp > 0.5, s_all - slope * (t - pos).astype(F32), NEG_INF)
    s_new = new_key_score(2 * GD)
    m = jnp.maximum(jnp.max(s_all, axis=1, keepdims=True), s_new)
    p_all = jnp.exp(s_all - m)
    p_new = jnp.exp(s_new - m)
    l = jnp.sum(p_all, axis=1, keepdims=True) + p_new
    p_all = p_all.astype(BF16)
    pv_s = []
    for j in toks:
        acc = None
        for i, pg_ref in enumerate(pages[j]):
            part = _dot_nt(p_all[j * N_HEADS:(j + 1) * N_HEADS, i * PAGE_SIZE:(i + 1) * PAGE_SIZE],
                           pg_ref[0, GD:, :].astype(BF16))
            acc = part if acc is None else acc + part
        pv_s.append(acc)
    o_s = (new_val(p_new, 3 * GD) + stack(pv_s)) / l

    grow = stack([per_head(g_ref[j]) for j in toks])
    h128 = lax.broadcasted_iota(jnp.int32, (rows, LANES), 0) % N_HEADS
    c128 = lax.broadcasted_iota(jnp.int32, (rows, LANES), 1)

    def gate(br):
        return jnp.sum(jnp.where(c128 == 3 * h128 + br, grow, 0.0), axis=1, keepdims=True)

    o = jnp.where(own, gate(0) * o_c + gate(1) * o_s + gate(2) * o_w, 0.0)
    oh = (o[:, 0:HEAD_DIM] + o[:, HEAD_DIM:2 * HEAD_DIM]
          + o[:, 2 * HEAD_DIM:3 * HEAD_DIM] + o[:, 3 * HEAD_DIM:4 * HEAD_DIM])
    for j in toks:
        o_ref[j] = jnp.concatenate([oh[j * N_HEADS + h:j * N_HEADS + h + 1, :] for h in range(N_HEADS)], axis=1)


def _nsa_sample(page_table, q_s, g_s, kc, vct, kvn, wcol, state_t, cache_t, slopes, ovl, expand, emit_window):
    per_b = lambda *shape: pl.BlockSpec((NB_S,) + shape, lambda b, pt: (b,) + (0,) * len(shape))
    tok_page = lambda j, p: _page_spec(NB_S, j, p, 1)
    n_out = 2 if emit_window else 1
    return pl.pallas_call(
        _nsa_sample_kernel,
        grid_spec=pltpu.PrefetchScalarGridSpec(
            num_scalar_prefetch=1,
            grid=(DEC_BATCH // NB_S,),
            in_specs=[
                per_b(1, D_MODEL), per_b(1, LANES), per_b(N_CMP_PAD, GD), per_b(GD, N_CMP_PAD),
                per_b(1, KV_ROW + KV_WIN), _const_sp((KV_WIN, DEC_BATCH)), per_b(KV_WIN, WINDOW),
            ] + [tok_page(j, p) for j in range(NB_S) for p in range(N_PAGES)] + [
                _const_sp((N_HEADS, LANES)), _const_sp((N_CMP_PAD, LANES)), _const_sp((LANES, PAST_LEN)),
            ],
            out_specs=[per_b(1, D_MODEL), per_b(KV_WIN, WINDOW)][:n_out],
        ),
        out_shape=[
            jax.ShapeDtypeStruct((DEC_BATCH, 1, D_MODEL), F32),
            jax.ShapeDtypeStruct((DEC_BATCH, KV_WIN, WINDOW), F32),
        ][:n_out],
        compiler_params=_cparams(1),
    )(page_table, q_s, g_s, kc, vct, kvn, wcol, state_t, *([cache_t] * (NB_S * N_PAGES)), slopes, ovl, expand)


def _outproj_sample_kernel(o_ref, x_ref, w_ref, xo_ref):
    xo_ref[...] = x_ref[...] + _dot(o_ref[...].astype(BF16), w_ref[...])


def _outproj_sample(o_pad, x, w_out):
    return pl.pallas_call(
        _outproj_sample_kernel,
        grid=(1,),
        in_specs=[
            pl.BlockSpec((TM, D_MODEL), lambda i: (0, 0)),
            pl.BlockSpec((TM, D_MODEL), lambda i: (N_PT, 0)),
            _const((D_MODEL, D_MODEL)),
        ],
        out_specs=pl.BlockSpec((TM, D_MODEL), lambda i: (N_PT, 0)),
        out_shape=jax.ShapeDtypeStruct((N_TOT, D_MODEL), F32),
        input_output_aliases={1: 0},
        compiler_params=_cparams(1),
    )(o_pad, x, w_out)


def _final_norm_kernel(x_ref, nrm_ref, yp_ref, ys_ref):
    i = pl.program_id(0)
    y = _rms(x_ref[...], nrm_ref[...])

    @pl.when(i < N_PT)
    def _():
        yp_ref[...] = y

    @pl.when(i == N_PT)
    def _():
        ys_ref[...] = y[:DEC_BATCH, :]


def _final_norm(x, nrm):
    return pl.pallas_call(
        _final_norm_kernel,
        grid=(N_TILES,),
        in_specs=[pl.BlockSpec((TM, D_MODEL), lambda i: (i, 0)), _const((1, D_MODEL))],
        out_specs=[
            pl.BlockSpec((TM, D_MODEL), lambda i: (jnp.minimum(i, N_PT - 1), 0)),
            pl.BlockSpec((DEC_BATCH, D_MODEL), lambda i: (0, 0)),
        ],
        out_shape=[
            jax.ShapeDtypeStruct((N_P, D_MODEL), F32),
            jax.ShapeDtypeStruct((DEC_BATCH, D_MODEL), F32),
        ],
        compiler_params=_cparams(1),
    )(x, nrm)


def _row(v):
    return v.reshape(1, -1).astype(F32)


def _prep_mix(w_s, b_s):
    causal = jnp.tril(jnp.ones((CHUNK, CHUNK), F32))
    eye = jnp.eye(CHUNK, dtype=F32)
    w0 = w_s * causal
    w1 = w_s[:, 0:1, 0:1] * eye
    b0 = jnp.repeat(b_s.T, D_A // A_GROUPS, axis=1)
    b1 = jnp.broadcast_to(jnp.repeat(b_s[:, 0], D_A // A_GROUPS)[None, :], (CHUNK, D_A))
    return jnp.stack([w0, w1]).astype(BF16), jnp.stack([b0, b1]).astype(F32)


def _prep_compress(cmp_pe, cmp_w1, cmp_w2):
    eye = jnp.eye(N_KV_GROUPS, dtype=F32)
    w1 = cmp_w1.reshape(2, 2, CMP_STRIDE, HEAD_DIM, HEAD_DIM)
    ws = jnp.einsum('khrde,gj->krgdhje', w1, eye).reshape(2, CMP_STRIDE * GD, 2 * GD).astype(BF16)
    cvec = jnp.einsum('kld,klde->ke', cmp_pe, cmp_w1, precision=HIGHEST)
    cvec = jnp.tile(cvec, (1, N_KV_GROUPS)).reshape(2, 1, GD).astype(F32)
    w2bd = jnp.einsum('ked,gj->kgejd', cmp_w2, eye).reshape(2, GD, GD).astype(BF16)
    return ws, cvec, w2bd


def _overlap():
    n = np.arange(N_CMP_PAD)[:, None] * CMP_STRIDE
    s0 = np.arange(LANES)[None, :] * SLC_BLOCK
    ovl = ((n < s0 + SLC_BLOCK) & (n + CMP_LEN > s0)).astype(np.float32)
    ovl[N_CMP_PAD - 1, :] = 0.0
    return ovl


def _position_minor(a):
    return jnp.transpose(a, (0, 2, 3, 4, 1)).reshape(a.shape[0], -1, a.shape[1])


def _token_major(a, n_kinds):
    a = a.reshape(a.shape[0], n_kinds, N_KV_GROUPS, HEAD_DIM, a.shape[-1])
    return jnp.transpose(a, (0, 4, 1, 2, 3))


def kernel(x_prompt, x_sample, cache_kv, state_win_kv, page_table, norm_mix, norm_ffn, norm_final, a_w_in, a_ln_g, a_ln_b, a_w_s, a_b_s, a_w_out, kv_norm, w_kv, cmp_pe, cmp_w1, cmp_w2, b_w_in, b_b_gate, b_w_out, f_w_gate, f_w_up, f_w_down, m_w_router, m_b_router, m_w_gate, m_w_up, m_w_down):
    x = jnp.concatenate([x_prompt.reshape(N_P, D_MODEL), x_sample.reshape(DEC_BATCH, D_MODEL),
                         jnp.zeros((TM - DEC_BATCH, D_MODEL), F32)], axis=0)
    cache_t = _position_minor(cache_kv)
    state_t = _position_minor(state_win_kv)
    ovl = _overlap()
    ovl_s = jnp.asarray(ovl)
    ovl_pt = jnp.asarray(ovl[:, :N_SEL_P].T.copy())
    expand = jnp.asarray((np.arange(PAST_LEN)[None, :] // SLC_BLOCK == np.arange(LANES)[:, None])
                         .astype(np.float32)).astype(BF16)
    slopes = jnp.asarray(np.repeat(np.asarray(_SLOPES, np.float32)[:, None], LANES, axis=1))
    moe_w = [w.astype(BF16) for w in (m_w_gate, m_w_up, m_w_down)]

    v_p, v_s = [], []
    win_s = None
    for layer in range(DEPTH):
        if layer == N_A_LAYERS:
            wkv = w_kv.astype(BF16)
            (rowst_p, wint_p, kvt_s, kvn_s, cmp_nat,
             ksel, kwin, vselt, vwint) = _kvproj(x, _row(kv_norm), wkv, wkv.T)
            ws, cvec, w2bd = _prep_compress(cmp_pe, cmp_w1, cmp_w2)
            kc_p, vct_p = _compress_prompt(cmp_nat, ws, cvec, w2bd)
            kc_s, vct_s = _compress_sample(page_table, cache_t, ws, cvec, w2bd)
        if layer < N_A_LAYERS:
            wmix, bias = _prep_mix(a_w_s[layer], a_b_s[layer])
            x, vp, vs = _a_mixer(x, _row(norm_mix[layer]), a_w_in[layer].astype(BF16), _row(a_ln_g[layer]),
                                 _row(a_ln_b[layer]), wmix, bias, a_w_out[layer].astype(BF16))
            v_p.append(vp)
            v_s.append(vs)
        else:
            i = layer - N_A_LAYERS
            w_in = jnp.pad(b_w_in[i], ((0, 0), (0, LANES - 3 * N_HEADS))).astype(BF16)
            bg = jnp.pad(b_b_gate[i], (0, LANES - 3 * N_HEADS)).reshape(1, LANES).astype(F32)
            w_out = b_w_out[i].astype(BF16)
            q, gates = _qproj(x, _row(norm_mix[layer]), w_in, bg)
            q_s = q[N_P:N_P + DEC_BATCH].reshape(DEC_BATCH, 1, D_MODEL)
            g_s = gates[N_P:N_P + DEC_BATCH].reshape(DEC_BATCH, 1, LANES)
            outs = _nsa_sample(page_table, q_s, g_s, kc_s, vct_s,
                               kvn_s.reshape(DEC_BATCH, 1, KV_ROW + KV_WIN), kvt_s[KV_ROW:],
                               state_t, cache_t, slopes, ovl_s, expand, emit_window=(i == 0))
            o_s = outs[0]
            if i == 0:
                win_s = outs[1]
            x = _nsa_prompt(q, gates, x, kc_p, vct_p, ksel, vselt, kwin, vwint, ovl_pt, w_out)
            o_pad = jnp.pad(o_s.reshape(DEC_BATCH, D_MODEL), ((0, TM - DEC_BATCH), (0, 0)))
            x = _outproj_sample(o_pad, x, w_out)
        j = layer // 2
        if layer % 2 == 0:
            x = _ffn(x, _row(norm_ffn[layer]), f_w_gate[j].astype(BF16), f_w_up[j].astype(BF16),
                     f_w_down[j].astype(BF16))
        else:
            wr = jnp.pad(m_w_router[j], ((0, 0), (0, LANES - N_EXPERTS))).astype(F32)
            br = jnp.pad(m_b_router[j], (0, LANES - N_EXPERTS), constant_values=NEG_INF).reshape(1, LANES)
            x = _moe_layer(x, _row(norm_ffn[layer]), wr, br.astype(F32), *moe_w, j)
    y_p, y_s = _final_norm(x, _row(norm_final))

    return (y_p.reshape(BATCH, SEQ, D_MODEL),
            y_s.reshape(DEC_BATCH, 1, D_MODEL),
            _token_major(rowst_p, 4),
            _token_major(kvt_s[:KV_ROW].reshape(1, KV_ROW, DEC_BATCH), 4).reshape(
                DEC_BATCH, 1, 4, N_KV_GROUPS, HEAD_DIM),
            _token_major(wint_p[:, :, SEQ - WINDOW:], 2),
            _token_major(win_s, 2),
            jnp.stack(v_p),
            jnp.stack(v_s).reshape(N_A_LAYERS, DEC_BATCH, 1, D_A))
```

```python
import functools

import numpy as np
import jax
import jax.numpy as jnp
from jax import lax
from jax.experimental import pallas as pl
from jax.experimental.pallas import tpu as pltpu

F32 = jnp.float32
BF16 = jnp.bfloat16
HIGHEST = lax.Precision.HIGHEST

D_MODEL = 1024
BATCH = 8
SEQ = 2048
DEPTH = 4
DEC_BATCH = 128
PAST_LEN = 2048
PAGE_SIZE = 128
N_A_LAYERS = DEPTH // 2
CHUNK = 128
D_A = D_MODEL
A_GROUPS = 8
N_HEADS = 16
HEAD_DIM = 64
N_KV_GROUPS = 4
Q_PER_GROUP = 4
CMP_LEN = 32
CMP_STRIDE = 16
SLC_BLOCK = 64
N_TOP = 16
WINDOW = 512
D_FF = 2816
N_EXPERTS = 8
TOP_K = 2
RMS_EPS = 1e-6
LN_EPS = 1e-5
NEG_INF = -1e30
FORCE_SCORE = 1e4

LANES = 128
TM = 512
N_P = BATCH * SEQ
N_PT = N_P // TM
N_TOT = N_P + TM
N_TILES = N_TOT // TM
N_REAL = N_P + DEC_BATCH
TQ = 128
BK_SEL = 256
BK_WIN = 128
N_WIN_KEYS = WINDOW + TQ
N_CMP_PAD = 128
N_SEL_P = SEQ // SLC_BLOCK
N_PAGES = PAST_LEN // PAGE_SIZE
NB_S = 2
NB_S_LATE = 4
NB_C = 2
GD = N_KV_GROUPS * HEAD_DIM
KV_ROW = 4 * GD
KV_WIN = 2 * GD
T_MOE = 512
N_ASSIGN = N_REAL * TOP_K
N_MOE_BLOCKS = -(-N_ASSIGN // T_MOE) + N_EXPERTS
N_MOE_ROWS = N_MOE_BLOCKS * T_MOE
FF_SPLIT = 2
VMEM_LIMIT = 56 * 1024 * 1024

_SLOPES = [float(v) for v in
           (2.0 ** (-8.0 * np.arange(1, N_HEADS + 1, dtype=np.float32) / N_HEADS)).astype(np.float32)]

K_AUG = 2 * HEAD_DIM
AUG_ROWS = 16
AUG_ONE, AUG_HI, AUG_LO, AUG_SEL = 0, 3, 6, 9
POS_LO = 128


def _split3_const(v):
    out = []
    r = np.float32(v)
    for _ in range(3):
        p = np.float32(np.asarray(r, dtype=jnp.bfloat16))
        out.append(float(p))
        r = np.float32(r - p)
    return out


LOG2E = float(np.log2(np.e))
_SLOPES2 = [float(np.float32(np.float32(v) * np.float32(LOG2E))) for v in _SLOPES]
_SLOPE_SPLIT = [_split3_const(v) for v in _SLOPES2]


def _cparams(n_axes):
    return pltpu.CompilerParams(dimension_semantics=("arbitrary",) * n_axes,
                                vmem_limit_bytes=VMEM_LIMIT)


def _const(shape):
    nd = len(shape)
    return pl.BlockSpec(shape, lambda *_: (0,) * nd, pipeline_mode=pl.Buffered(1))


def _rms(x, g):
    return x * lax.rsqrt(jnp.mean(x * x, axis=-1, keepdims=True) + RMS_EPS) * g


def _dot(a, b, **kw):
    return jnp.dot(a, b, preferred_element_type=F32, **kw)


def _dot_nt(a, b, **kw):
    return lax.dot_general(a, b, (((1,), (1,)), ((), ())), preferred_element_type=F32, **kw)


def _a_mixer_kernel(x_ref, nrm_ref, win_ref, lng_ref, lnb_ref, wmix_ref, bias_ref, wout_ref,
                    xo_ref, vp_ref, vs_ref, mixed_ref):
    i = pl.program_id(0)
    x = x_ref[...]
    h = _rms(x, nrm_ref[...]).astype(BF16)
    z = jax.nn.gelu(_dot(h, win_ref[...]))
    u = z[:, :D_A]
    v = z[:, D_A:]
    mu = jnp.mean(v, axis=-1, keepdims=True)
    var = jnp.mean(jnp.square(v - mu), axis=-1, keepdims=True)
    v = (v - mu) * lax.rsqrt(var + LN_EPS) * lng_ref[...] + lnb_ref[...]

    @pl.when((i < N_PT) & (i % (SEQ // TM) == SEQ // TM - 1))
    def _():
        vp_ref[0] = v[TM - CHUNK:, :]

    @pl.when(i == N_PT)
    def _():
        vs_ref[...] = v[:DEC_BATCH, :]

    vb = v.astype(BF16)
    for c in range(TM // CHUNK):
        for g in range(A_GROUPS):
            cols = slice(g * LANES, (g + 1) * LANES)
            rows = slice(c * CHUNK, (c + 1) * CHUNK)
            mixed_ref[rows, cols] = _dot(wmix_ref[0, g], vb[rows, cols]) + bias_ref[0, :, cols]
    t = (u * mixed_ref[...]).astype(BF16)
    xo_ref[...] = x + _dot(t, wout_ref[...])


def _a_mixer(x, nrm, w_in, ln_g, ln_b, wmix, bias, w_out):
    return pl.pallas_call(
        _a_mixer_kernel,
        grid=(N_TILES,),
        in_specs=[
            pl.BlockSpec((TM, D_MODEL), lambda i: (i, 0)),
            _const((1, D_MODEL)),
            _const((D_MODEL, 2 * D_A)),
            _const((1, D_A)),
            _const((1, D_A)),
            pl.BlockSpec((1, A_GROUPS, CHUNK, CHUNK), lambda i: (i // N_PT, 0, 0, 0)),
            pl.BlockSpec((1, CHUNK, D_A), lambda i: (i // N_PT, 0, 0)),
            _const((D_A, D_MODEL)),
        ],
        out_specs=[
            pl.BlockSpec((TM, D_MODEL), lambda i: (i, 0)),
            pl.BlockSpec((1, CHUNK, D_A), lambda i: (jnp.minimum(i // (SEQ // TM), BATCH - 1), 0, 0)),
            pl.BlockSpec((DEC_BATCH, D_A), lambda i: (0, 0)),
        ],
        out_shape=[
            jax.ShapeDtypeStruct((N_TOT, D_MODEL), F32),
            jax.ShapeDtypeStruct((BATCH, CHUNK, D_A), F32),
            jax.ShapeDtypeStruct((DEC_BATCH, D_A), F32),
        ],
        scratch_shapes=[pltpu.VMEM((TM, D_A), F32)],
        compiler_params=_cparams(1),
    )(x, nrm, w_in, ln_g, ln_b, wmix, bias, w_out)


def _swiglu_block(h, wg_ref, wu_ref, wd_ref, lead):
    ffh = D_FF // FF_SPLIT
    out = None
    for s in range(FF_SPLIT):
        cols = slice(s * ffh, (s + 1) * ffh)
        g = _dot(h, wg_ref[lead + (slice(None), cols)])
        u = _dot(h, wu_ref[lead + (slice(None), cols)])
        a = (jax.nn.silu(g) * u).astype(BF16)
        part = _dot(a, wd_ref[lead + (cols, slice(None))])
        out = part if out is None else out + part
    return out


def _ffn_kernel(x_ref, nrm_ref, wg_ref, wu_ref, wd_ref, o_ref):
    x = x_ref[...]
    h = _rms(x, nrm_ref[...]).astype(BF16)
    o_ref[...] = x + _swiglu_block(h, wg_ref, wu_ref, wd_ref, ())


def _ffn(x, nrm, wg, wu, wd):
    return pl.pallas_call(
        _ffn_kernel,
        grid=(N_TILES,),
        in_specs=[
            pl.BlockSpec((TM, D_MODEL), lambda i: (i, 0)),
            _const((1, D_MODEL)),
            _const((D_MODEL, D_FF)),
            _const((D_MODEL, D_FF)),
            _const((D_FF, D_MODEL)),
        ],
        out_specs=pl.BlockSpec((TM, D_MODEL), lambda i: (i, 0)),
        out_shape=jax.ShapeDtypeStruct((N_TOT, D_MODEL), F32),
        compiler_params=_cparams(1),
    )(x, nrm, wg, wu, wd)


def _router_kernel(x_ref, nrm_ref, wr_ref, br_ref, h_ref, r_ref):
    h = _rms(x_ref[...], nrm_ref[...])
    h_ref[...] = h
    logits = _dot(h, wr_ref[...], precision=HIGHEST) + br_ref[...]
    lane = lax.broadcasted_iota(jnp.int32, logits.shape, 1).astype(F32)
    big = float(LANES)
    m1 = jnp.max(logits, axis=1, keepdims=True)
    i1 = jnp.min(jnp.where(logits == m1, lane, big), axis=1, keepdims=True)
    l2 = jnp.where(lane == i1, -jnp.inf, logits)
    m2 = jnp.max(l2, axis=1, keepdims=True)
    i2 = jnp.min(jnp.where(l2 == m2, lane, big), axis=1, keepdims=True)
    e = jnp.exp(m2 - m1)
    g1 = 1.0 / (1.0 + e)
    g2 = e / (1.0 + e)
    r_ref[...] = jnp.where(lane == 0.0, i1, jnp.where(lane == 1.0, i2,
                           jnp.where(lane == 2.0, g1, jnp.where(lane == 3.0, g2, 0.0))))


def _router(x, nrm, wr, br):
    return pl.pallas_call(
        _router_kernel,
        grid=(N_TILES,),
        in_specs=[
            pl.BlockSpec((TM, D_MODEL), lambda i: (i, 0)),
            _const((1, D_MODEL)),
            _const((D_MODEL, LANES)),
            _const((1, LANES)),
        ],
        out_specs=[
            pl.BlockSpec((TM, D_MODEL), lambda i: (i, 0)),
            pl.BlockSpec((TM, LANES), lambda i: (i, 0)),
        ],
        out_shape=[
            jax.ShapeDtypeStruct((N_TOT, D_MODEL), F32),
            jax.ShapeDtypeStruct((N_TOT, LANES), F32),
        ],
        compiler_params=_cparams(1),
    )(x, nrm, wr, br)


def _moe_kernel(be_ref, na_ref, rt_ref, h_hbm, wg_ref, wu_ref, wd_ref, o_ref, xbuf, sem):
    i = pl.program_id(0)
    na = na_ref[0]
    slot = i % 2

    def start_gather(blk, s):
        base = blk * T_MOE

        def body(r, carry):
            pltpu.make_async_copy(h_hbm.at[rt_ref[base + r]], xbuf.at[s, r], sem.at[s]).start()
            return carry

        lax.fori_loop(0, T_MOE, body, 0, unroll=8)

    @pl.when(i == 0)
    def _():
        start_gather(0, 0)

    @pl.when(i + 1 < na)
    def _():
        start_gather(i + 1, 1 - slot)

    @pl.when(i < na)
    def _():
        pltpu.make_async_copy(h_hbm.at[pl.ds(0, T_MOE)], xbuf.at[slot], sem.at[slot]).wait()
        x = jnp.concatenate([xbuf[slot, :, c, :] for c in range(D_MODEL // LANES)], axis=1).astype(BF16)
        o_ref[...] = _swiglu_block(x, wg_ref, wu_ref, wd_ref, (0, 0))

    @pl.when(i >= na)
    def _():
        o_ref[...] = jnp.zeros(o_ref.shape, o_ref.dtype)


def _moe_experts(blk_expert, n_active, row_token, h_tiles, wg, wu, wd, layer):
    return pl.pallas_call(
        _moe_kernel,
        grid_spec=pltpu.PrefetchScalarGridSpec(
            num_scalar_prefetch=3,
            grid=(N_MOE_BLOCKS,),
            in_specs=[
                pl.BlockSpec(memory_space=pl.ANY),
                pl.BlockSpec((1, 1, D_MODEL, D_FF), lambda i, be, na, rt: (layer, be[i], 0, 0)),
                pl.BlockSpec((1, 1, D_MODEL, D_FF), lambda i, be, na, rt: (layer, be[i], 0, 0)),
                pl.BlockSpec((1, 1, D_FF, D_MODEL), lambda i, be, na, rt: (layer, be[i], 0, 0)),
            ],
            out_specs=pl.BlockSpec((T_MOE, D_MODEL), lambda i, be, na, rt: (i, 0)),
            scratch_shapes=[
                pltpu.VMEM((2, T_MOE, D_MODEL // LANES, LANES), F32),
                pltpu.SemaphoreType.DMA((2,)),
            ],
        ),
        out_shape=jax.ShapeDtypeStruct((N_MOE_ROWS, D_MODEL), F32),
        compiler_params=_cparams(1),
    )(blk_expert, n_active, row_token, h_tiles, wg, wu, wd)


def _moe_layer(x, nrm, wr, br, wg, wu, wd, layer):
    h, r = _router(x, nrm, wr, br)
    r = r[:N_REAL]
    expert = r[:, 0:2].astype(jnp.int32).reshape(N_ASSIGN)
    gate = r[:, 2:4]
    onehot = (expert[:, None] == jnp.arange(N_EXPERTS, dtype=jnp.int32)[None, :]).astype(jnp.int32)
    csum = jnp.cumsum(onehot, axis=0)
    counts = csum[-1]
    rank = jnp.sum(csum * onehot, axis=1) - 1
    padded = (counts + T_MOE - 1) // T_MOE * T_MOE
    pad_end = jnp.cumsum(padded)
    pad_start = pad_end - padded
    dest = jnp.sum(pad_start[None, :] * onehot, axis=1) + rank
    blk_start = jnp.arange(N_MOE_BLOCKS, dtype=jnp.int32) * T_MOE
    blk_expert = jnp.minimum(jnp.sum(pad_end[None, :] <= blk_start[:, None], axis=1),
                             N_EXPERTS - 1).astype(jnp.int32)
    n_active = (pad_end[-1:] // T_MOE).astype(jnp.int32)
    n_pad = N_MOE_ROWS - N_ASSIGN
    gaps = padded - counts
    gap_end = jnp.cumsum(gaps)
    gap_start = gap_end - gaps
    p = jnp.arange(n_pad, dtype=jnp.int32)
    e_p = jnp.sum(gap_end[None, :] <= p[:, None], axis=1)
    hot_p = (e_p[:, None] == jnp.arange(N_EXPERTS, dtype=jnp.int32)[None, :]).astype(jnp.int32)
    row_in = jnp.sum(hot_p * (pad_start + counts - gap_start)[None, :], axis=1) + p
    pad_rows = jnp.where(e_p < N_EXPERTS, row_in, pad_end[-1] + p - gap_end[-1])
    keys = jnp.concatenate([dest, pad_rows]).astype(jnp.int32)
    vals = jnp.concatenate([jnp.arange(N_ASSIGN, dtype=jnp.int32) // TOP_K, jnp.full((n_pad,), N_REAL, jnp.int32)])
    _, row_token = lax.sort_key_val(keys, vals)
    h_tiles = h.reshape(N_TOT, D_MODEL // LANES, LANES)
    y_rows = _moe_experts(blk_expert, n_active, row_token, h_tiles, wg, wu, wd, layer)
    d2 = dest.reshape(N_REAL, TOP_K)
    y = gate[:, 0:1] * y_rows[d2[:, 0]] + gate[:, 1:2] * y_rows[d2[:, 1]]
    return x.at[:N_REAL].add(y)


def _kvproj_kernel(x_ref, nrm_ref, wkv_ref, wkvt_ref, rowst_ref, wint_ref, kvt_s_ref, kvn_s_ref, cmp_ref,
                   ksel_ref, kwin_ref, vselt_ref, vwint_ref):
    i = pl.program_id(0)
    h = _rms(x_ref[...], nrm_ref[...]).astype(BF16)
    kv = _dot(h, wkv_ref[...])
    kvt = _dot_nt(wkvt_ref[...], h)

    @pl.when(i < N_PT)
    def _():
        rowst_ref[0] = kvt[:KV_ROW, :]
        wint_ref[0] = kvt[KV_ROW:, :]
        for cb in range(2 * GD // LANES):
            cmp_ref[cb] = kv[:, cb * LANES:(cb + 1) * LANES]

    @pl.when(i == N_PT)
    def _():
        kvt_s_ref[...] = kvt[:, :DEC_BATCH]
        kvn_s_ref[...] = kv[:DEC_BATCH, :]

    kvt_b = kvt.astype(BF16)
    pos = (i % (SEQ // TM)) * TM + lax.broadcasted_iota(jnp.int32, (TM, HEAD_DIM), 0)
    a_col = lax.broadcasted_iota(jnp.int32, (TM, HEAD_DIM), 1)
    hi = (pos // POS_LO * POS_LO).astype(F32)
    lo = (pos % POS_LO).astype(F32)
    aug = jnp.where(a_col < AUG_HI, 1.0, jnp.where(a_col < AUG_LO, hi, jnp.where(a_col < AUG_SEL, lo, 0.0)))
    sel_hot = jnp.where((a_col >= AUG_SEL) & (a_col - AUG_SEL == (pos // SLC_BLOCK) % (BK_SEL // SLC_BLOCK)),
                        1.0, 0.0)
    for g in range(N_KV_GROUPS):
        k_s = kv[:, 2 * GD + g * HEAD_DIM:2 * GD + (g + 1) * HEAD_DIM]
        k_w = kv[:, 4 * GD + g * HEAD_DIM:4 * GD + (g + 1) * HEAD_DIM]
        ksel_ref[g] = jnp.concatenate([k_s, aug + sel_hot], axis=1).astype(BF16)
        kwin_ref[g] = jnp.concatenate([k_w, aug], axis=1).astype(BF16)
        for j in range(TM // BK_SEL):
            vselt_ref[g, j] = kvt_b[3 * GD + g * HEAD_DIM:3 * GD + (g + 1) * HEAD_DIM,
                                    j * BK_SEL:(j + 1) * BK_SEL]
        for j in range(TM // BK_WIN):
            vwint_ref[g, j] = kvt_b[5 * GD + g * HEAD_DIM:5 * GD + (g + 1) * HEAD_DIM,
                                    j * BK_WIN:(j + 1) * BK_WIN]


def _kvproj(x, nrm, wkv, wkvt):
    n_kv = KV_ROW + KV_WIN
    tiles_per_seq = SEQ // TM
    ip = lambda i: jnp.minimum(i, N_PT - 1)
    return pl.pallas_call(
        _kvproj_kernel,
        grid=(N_TILES,),
        in_specs=[
            pl.BlockSpec((TM, D_MODEL), lambda i: (i, 0)),
            _const((1, D_MODEL)),
            _const((D_MODEL, n_kv)),
            _const((n_kv, D_MODEL)),
        ],
        out_specs=[
            pl.BlockSpec((1, KV_ROW, TM), lambda i: (ip(i) // tiles_per_seq, 0, ip(i) % tiles_per_seq)),
            pl.BlockSpec((1, KV_WIN, TM), lambda i: (ip(i) // tiles_per_seq, 0, ip(i) % tiles_per_seq)),
            pl.BlockSpec((n_kv, DEC_BATCH), lambda i: (0, 0)),
            pl.BlockSpec((DEC_BATCH, n_kv), lambda i: (0, 0)),
            pl.BlockSpec((2 * GD // LANES, TM, LANES), lambda i: (0, ip(i), 0)),
            pl.BlockSpec((N_KV_GROUPS, TM, K_AUG), lambda i: (0, i, 0)),
            pl.BlockSpec((N_KV_GROUPS, TM, K_AUG), lambda i: (0, i, 0)),
            pl.BlockSpec((N_KV_GROUPS, TM // BK_SEL, HEAD_DIM, BK_SEL), lambda i: (0, i, 0, 0)),
            pl.BlockSpec((N_KV_GROUPS, TM // BK_WIN, HEAD_DIM, BK_WIN), lambda i: (0, i, 0, 0)),
        ],
        out_shape=[
            jax.ShapeDtypeStruct((BATCH, KV_ROW, SEQ), F32),
            jax.ShapeDtypeStruct((BATCH, KV_WIN, SEQ), F32),
            jax.ShapeDtypeStruct((n_kv, DEC_BATCH), F32),
            jax.ShapeDtypeStruct((DEC_BATCH, n_kv), F32),
            jax.ShapeDtypeStruct((2 * GD // LANES, N_P, LANES), F32),
            jax.ShapeDtypeStruct((N_KV_GROUPS, N_TOT, K_AUG), BF16),
            jax.ShapeDtypeStruct((N_KV_GROUPS, N_TOT, K_AUG), BF16),
            jax.ShapeDtypeStruct((N_KV_GROUPS, N_TOT // BK_SEL, HEAD_DIM, BK_SEL), BF16),
            jax.ShapeDtypeStruct((N_KV_GROUPS, N_TOT // BK_WIN, HEAD_DIM, BK_WIN), BF16),
        ],
        compiler_params=_cparams(1),
    )(x, nrm, wkv, wkvt)


def _compress_body(nat_ref, leads, ws_ref, c_ref, w2_ref, kc_ref, vct_ref, prepare=None):
    n_seg = SEQ // CMP_STRIDE
    for kind in range(2):
        if prepare is not None:
            prepare(kind)
        y = None
        for r in range(CMP_STRIDE):
            xr = jnp.concatenate([jnp.concatenate(
                [nat_ref[lead + (kind * (GD // LANES) + j, pl.ds(r, n_seg, stride=CMP_STRIDE), slice(None))]
                 for j in range(GD // LANES)], axis=1) for lead in leads], axis=0)
            part = _dot(xr.astype(BF16), ws_ref[kind, r * GD:(r + 1) * GD, :])
            y = part if y is None else y + part
        a = y[:, :GD]
        b = pltpu.roll(y[:, GD:], len(leads) * N_CMP_PAD - 1, axis=0)
        hid = jax.nn.gelu(a + b + c_ref[kind])
        out = _dot(hid.astype(BF16), w2_ref[kind])
        for i in range(len(leads)):
            blocks = out[i * N_CMP_PAD:(i + 1) * N_CMP_PAD, :]
            if kind == 0:
                kc_ref[i] = blocks.astype(BF16)
            else:
                vct_ref[i] = blocks.T.astype(BF16)


def _compress_prompt_kernel(cmp_ref, ws_ref, c_ref, w2_ref, kc_ref, vct_ref):
    _compress_body(cmp_ref, [()], ws_ref, c_ref, w2_ref, kc_ref, vct_ref)


def _compress_sample_kernel(pt_ref, *refs):
    n_pg = NB_C * N_PAGES
    pages = [refs[j * N_PAGES:(j + 1) * N_PAGES] for j in range(NB_C)]
    ws_ref, c_ref, w2_ref, kc_ref, vct_ref, nat_ref = refs[n_pg:]

    def to_token_major(kind):
        for j in range(NB_C):
            for p, page in enumerate(pages[j]):
                for cb in range(kind * (GD // LANES), (kind + 1) * (GD // LANES)):
                    nat_ref[j, cb, p * PAGE_SIZE:(p + 1) * PAGE_SIZE, :] = page[0, cb * LANES:(cb + 1) * LANES, :].T

    _compress_body(nat_ref, [(j,) for j in range(NB_C)], ws_ref, c_ref, w2_ref, kc_ref, vct_ref,
                   prepare=to_token_major)


_CMP_W_SHAPES = [(2, CMP_STRIDE * GD, 2 * GD), (2, 1, GD), (2, GD, GD)]


def _compress_prompt(cmp_nat, ws, cvec, w2bd):
    return pl.pallas_call(
        _compress_prompt_kernel,
        grid=(BATCH,),
        in_specs=[pl.BlockSpec((2 * GD // LANES, SEQ, LANES), lambda b: (0, b, 0))]
        + [_const(s) for s in _CMP_W_SHAPES],
        out_specs=[
            pl.BlockSpec((1, N_CMP_PAD, GD), lambda b: (b, 0, 0)),
            pl.BlockSpec((1, GD, N_CMP_PAD), lambda b: (b, 0, 0)),
        ],
        out_shape=[
            jax.ShapeDtypeStruct((BATCH, N_CMP_PAD, GD), BF16),
            jax.ShapeDtypeStruct((BATCH, GD, N_CMP_PAD), BF16),
        ],
        compiler_params=_cparams(1),
    )(cmp_nat, ws, cvec, w2bd)


def _page_spec(n_tok, j, p, half):
    return pl.BlockSpec((1, 2 * GD, PAGE_SIZE), lambda b, pt: (pt[b * n_tok + j, p], half, 0))


def _const_sp(shape):
    return pl.BlockSpec(shape, functools.partial(lambda nd, b, pt: (0,) * nd, len(shape)),
                        pipeline_mode=pl.Buffered(1))


def _compress_sample(page_table, cache_t, ws, cvec, w2bd):
    return pl.pallas_call(
        _compress_sample_kernel,
        grid_spec=pltpu.PrefetchScalarGridSpec(
            num_scalar_prefetch=1,
            grid=(DEC_BATCH // NB_C,),
            in_specs=[_page_spec(NB_C, j, p, 0) for j in range(NB_C) for p in range(N_PAGES)]
            + [_const_sp(s) for s in _CMP_W_SHAPES],
            out_specs=[
                pl.BlockSpec((NB_C, N_CMP_PAD, GD), lambda b, pt: (b, 0, 0)),
                pl.BlockSpec((NB_C, GD, N_CMP_PAD), lambda b, pt: (b, 0, 0)),
            ],
            scratch_shapes=[pltpu.VMEM((NB_C, 2 * GD // LANES, PAST_LEN, LANES), F32)],
        ),
        out_shape=[
            jax.ShapeDtypeStruct((DEC_BATCH, N_CMP_PAD, GD), BF16),
            jax.ShapeDtypeStruct((DEC_BATCH, GD, N_CMP_PAD), BF16),
        ],
        compiler_params=_cparams(1),
    )(page_table, *([cache_t] * (NB_C * N_PAGES)), ws, cvec, w2bd)


def _qproj_kernel(x_ref, nrm_ref, w_ref, bg_ref, q_ref, g_ref):
    h = _rms(x_ref[...], nrm_ref[...]).astype(BF16)
    p = _dot(h, w_ref[...])
    q_ref[...] = p[:, :D_MODEL] * (HEAD_DIM ** -0.5)
    g_ref[...] = jax.nn.sigmoid(p[:, D_MODEL:] + bg_ref[...])


def _qproj(x, nrm, w, bg):
    return pl.pallas_call(
        _qproj_kernel,
        grid=(N_TILES,),
        in_specs=[
            pl.BlockSpec((TM, D_MODEL), lambda i: (i, 0)),
            _const((1, D_MODEL)),
            _const((D_MODEL, D_MODEL + LANES)),
            _const((1, LANES)),
        ],
        out_specs=[
            pl.BlockSpec((TM, D_MODEL), lambda i: (i, 0)),
            pl.BlockSpec((TM, LANES), lambda i: (i, 0)),
        ],
        out_shape=[
            jax.ShapeDtypeStruct((N_TOT, D_MODEL), F32),
            jax.ShapeDtypeStruct((N_TOT, LANES), F32),
        ],
        compiler_params=_cparams(1),
    )(x, nrm, w, bg)


def _top_mask(score, n_sel, idx, axis):
    cnt = jnp.zeros(score.shape, F32)
    for i in range(n_sel):
        row = score[i:i + 1, :] if axis == 0 else score[:, i:i + 1]
        beats = (row > score) | ((row == score) & (idx > i))
        cnt = cnt + jnp.where(beats, 1.0, 0.0)
    return jnp.where(cnt < float(N_TOP), 0.0, NEG_INF)


def _split3(a):
    a1 = a.astype(BF16).astype(F32)
    r1 = a - a1
    a2 = r1.astype(BF16).astype(F32)
    a3 = (r1 - a2).astype(BF16).astype(F32)
    return a1, a2, a3


def _nsa_prompt_kernel(q_ref, gt_ref, x_ref, kc_ref, vct_ref, ks_ref, vst_ref, kw_ref, vwt_ref,
                       ovl_ref, wout_ref, o_ref, qgt_ref, aug_ref, st_ref, ot_ref, selneg_ref, m_ref, l_ref,
                       acc_ref):
    qi = pl.program_id(1)
    t0 = qi * TQ
    qt_all = q_ref[...].T
    gt = gt_ref[...].T
    tq = t0 + lax.broadcasted_iota(jnp.int32, (1, TQ), 1)
    tq_f = tq.astype(F32)
    groups = range(N_KV_GROUPS)
    heads = range(Q_PER_GROUP)
    lanes = [slice(r * TQ, (r + 1) * TQ) for r in heads]
    wide = Q_PER_GROUP * TQ

    n_idx = lax.broadcasted_iota(jnp.int32, (N_CMP_PAD, 1), 0)
    d_c = tq - (n_idx * CMP_STRIDE + CMP_LEN - 1)
    ok_c = d_c >= 0
    d_cf = d_c.astype(F32)
    j_idx = lax.broadcasted_iota(jnp.int32, (N_SEL_P, 1), 0)
    cur = tq // SLC_BLOCK
    valid = j_idx <= cur
    forced = (j_idx == 0) | (j_idx == cur) | (j_idx == cur - 1)

    rid = lax.broadcasted_iota(jnp.int32, (AUG_ROWS, TQ), 0)
    qc_t = []
    for g in groups:
        blocks = []
        for r in heads:
            hd = g * Q_PER_GROUP + r
            s1, s2, s3 = _SLOPE_SPLIT[hd]
            a1, a2, a3 = _split3(-_SLOPES2[hd] * tq_f)
            slope_rows = jnp.where((rid == AUG_HI) | (rid == AUG_LO), s1,
                                   jnp.where((rid == AUG_HI + 1) | (rid == AUG_LO + 1), s2,
                                             jnp.where((rid == AUG_HI + 2) | (rid == AUG_LO + 2), s3, 0.0)))
            blocks.append(jnp.where(rid == AUG_ONE, a1, jnp.where(rid == AUG_ONE + 1, a2,
                                    jnp.where(rid == AUG_ONE + 2, a3, slope_rows))))
        aug = jnp.concatenate(blocks, axis=1)
        aug_ref[g] = aug
        qg = jnp.concatenate(
            [qt_all[(g * Q_PER_GROUP + r) * HEAD_DIM:(g * Q_PER_GROUP + r + 1) * HEAD_DIM, :]
             for r in heads], axis=1)
        qc_t.append(qg.astype(BF16))
        qgt_ref[g, 0:HEAD_DIM, :] = (qg * LOG2E).astype(BF16)
        qgt_ref[g, HEAD_DIM:HEAD_DIM + AUG_ROWS, :] = aug.astype(BF16)
        qgt_ref[g, HEAD_DIM + AUG_ROWS:, :] = jnp.zeros((K_AUG - HEAD_DIM - AUG_ROWS, wide), BF16)

    oc_t = []
    sts_c = [_dot(kc_ref[0, :, g * HEAD_DIM:(g + 1) * HEAD_DIM], qc_t[g]) for g in groups]
    for g in groups:
        st = sts_c[g]
        psum = jnp.zeros((N_CMP_PAD, TQ), F32)
        ps = []
        for r in heads:
            s = st[:, lanes[r]] - _SLOPES[g * Q_PER_GROUP + r] * d_cf
            s = jnp.where(ok_c, s, NEG_INF)
            p = jnp.exp(s - jnp.max(s, axis=0, keepdims=True))
            p = p / jnp.sum(p, axis=0, keepdims=True)
            p = jnp.where(ok_c, p, 0.0)
            psum = psum + p
            ps.append(p.astype(BF16))
        oc_t.append(_dot(vct_ref[0, g * HEAD_DIM:(g + 1) * HEAD_DIM, :], jnp.concatenate(ps, axis=1)))
        imp = _dot(ovl_ref[...], psum, precision=HIGHEST)
        score = jnp.where(valid, jnp.where(forced, FORCE_SCORE, imp), NEG_INF)
        selneg_ref[g] = _top_mask(score, N_SEL_P, j_idx, 0)

    m_ref[...] = jnp.full(m_ref.shape, NEG_INF, F32)
    l_ref[...] = jnp.zeros(l_ref.shape, F32)
    acc_ref[...] = jnp.zeros(acc_ref.shape, F32)
    blocks_per_chunk = BK_SEL // SLC_BLOCK
    rid_w = lax.broadcasted_iota(jnp.int32, (AUG_ROWS, wide), 0)

    def score_group(c, g):
        k_rows = pl.ds(pl.multiple_of(c * BK_SEL, BK_SEL), BK_SEL)
        blk = aug_ref[g]
        for jj in range(blocks_per_chunk):
            row = selneg_ref[g, pl.ds(c * blocks_per_chunk + jj, 1), :]
            blk = jnp.where(rid_w == AUG_SEL + jj, jnp.concatenate([row] * Q_PER_GROUP, axis=1), blk)
        qgt_ref[g, HEAD_DIM:HEAD_DIM + AUG_ROWS, :] = blk.astype(BF16)
        st_ref[g] = _dot(ks_ref[g, k_rows, :], qgt_ref[g])

    def sel_chunk(c, diagonal):
        if diagonal:
            causal = (lax.broadcasted_iota(jnp.int32, (BK_SEL, TQ), 1)
                      - lax.broadcasted_iota(jnp.int32, (BK_SEL, TQ), 0) + (t0 - c * BK_SEL)) >= 0
        m_all = m_ref[...]
        l_all = l_ref[...]
        pvs, alpha_all, m_out, l_out = [], [], [], []
        for g in groups:
            ps, alphas, ms, ls = [], [], [], []
            for r in heads:
                s = st_ref[g, :, lanes[r]]
                if diagonal:
                    s = jnp.where(causal, s, NEG_INF)
                m_old = m_all[g, :, lanes[r]]
                m_new = jnp.maximum(m_old, jnp.max(s, axis=0, keepdims=True))
                alpha = jnp.exp2(m_old - m_new)
                p = jnp.exp2(s - m_new)
                ls.append(alpha * l_all[g, :, lanes[r]] + jnp.sum(p, axis=0, keepdims=True))
                ms.append(m_new)
                ps.append(p.astype(BF16))
                alphas.append(alpha)
            pvs.append(_dot(vst_ref[g, c], jnp.concatenate(ps, axis=1)))
            if not diagonal:
                score_group(c + 1, g)
            alpha_all.append(jnp.concatenate(alphas, axis=1))
            m_out.append(jnp.concatenate(ms, axis=1))
            l_out.append(jnp.concatenate(ls, axis=1))
        for g in groups:
            m_ref[g] = m_out[g]
            l_ref[g] = l_out[g]
            acc_ref[g] = acc_ref[g] * alpha_all[g] + pvs[g]

    def sel_body(c, carry):
        sel_chunk(c, False)
        return carry

    c_last = (t0 + TQ - 1) // BK_SEL
    for g in groups:
        score_group(0, g)
    lax.fori_loop(0, c_last, sel_body, 0)
    sel_chunk(c_last, True)

    for g in groups:
        qgt_ref[g, HEAD_DIM:HEAD_DIM + AUG_ROWS, :] = aug_ref[g].astype(BF16)
    k_start = pl.multiple_of(jnp.maximum(t0 - WINDOW, 0), BK_WIN)
    c_start = k_start // BK_WIN
    d_w = (lax.broadcasted_iota(jnp.int32, (N_WIN_KEYS, TQ), 1)
           - lax.broadcasted_iota(jnp.int32, (N_WIN_KEYS, TQ), 0)) + (t0 - k_start)
    madd_w = jnp.where((d_w >= 0) & (d_w < WINDOW), 0.0, NEG_INF)

    sts_w = [_dot(kw_ref[g, pl.ds(k_start, N_WIN_KEYS), :], qgt_ref[g]) for g in groups]
    for g in groups:
        st = sts_w[g]
        ps = []
        ls = []
        for r in heads:
            s = st[:, lanes[r]] + madd_w
            p = jnp.exp2(s - jnp.max(s, axis=0, keepdims=True))
            ls.append(jnp.sum(p, axis=0, keepdims=True))
            ps.append(p.astype(BF16))
        pt = jnp.concatenate(ps, axis=1)
        ow_t = None
        for c in range(N_WIN_KEYS // BK_WIN):
            part = _dot(vwt_ref[g, c_start + c], pt[c * BK_WIN:(c + 1) * BK_WIN, :])
            ow_t = part if ow_t is None else ow_t + part
        ow_t = ow_t / jnp.concatenate(ls, axis=1)
        os_t = acc_ref[g] / l_ref[g]
        for r in heads:
            hd = g * Q_PER_GROUP + r
            ot_ref[hd * HEAD_DIM:(hd + 1) * HEAD_DIM, :] = (
                gt[3 * hd:3 * hd + 1, :] * oc_t[g][:, lanes[r]]
                + gt[3 * hd + 1:3 * hd + 2, :] * os_t[:, lanes[r]]
                + gt[3 * hd + 2:3 * hd + 3, :] * ow_t[:, lanes[r]])

    o = ot_ref[...].T.astype(BF16)
    o_ref[...] = x_ref[...] + _dot(o, wout_ref[...])


def _nsa_prompt(q, gates, x, kc, vct, ksel, vselt, kwin, vwint, ovl_t, w_out):
    nq = SEQ // TQ
    tile = lambda b, qi: (b * nq + qi, 0)
    wide = Q_PER_GROUP * TQ
    return pl.pallas_call(
        _nsa_prompt_kernel,
        grid=(BATCH, nq),
        in_specs=[
            pl.BlockSpec((TQ, D_MODEL), tile),
            pl.BlockSpec((TQ, LANES), tile),
            pl.BlockSpec((TQ, D_MODEL), tile),
            pl.BlockSpec((1, N_CMP_PAD, GD), lambda b, qi: (b, 0, 0)),
            pl.BlockSpec((1, GD, N_CMP_PAD), lambda b, qi: (b, 0, 0)),
            pl.BlockSpec((N_KV_GROUPS, SEQ, K_AUG), lambda b, qi: (0, b, 0)),
            pl.BlockSpec((N_KV_GROUPS, SEQ // BK_SEL, HEAD_DIM, BK_SEL), lambda b, qi: (0, b, 0, 0)),
            pl.BlockSpec((N_KV_GROUPS, SEQ, K_AUG), lambda b, qi: (0, b, 0)),
            pl.BlockSpec((N_KV_GROUPS, SEQ // BK_WIN, HEAD_DIM, BK_WIN), lambda b, qi: (0, b, 0, 0)),
            _const((N_SEL_P, N_CMP_PAD)),
            _const((D_MODEL, D_MODEL)),
        ],
        out_specs=pl.BlockSpec((TQ, D_MODEL), tile),
        out_shape=jax.ShapeDtypeStruct((N_TOT, D_MODEL), F32),
        input_output_aliases={2: 0},
        scratch_shapes=[
            pltpu.VMEM((N_KV_GROUPS, K_AUG, wide), BF16),
            pltpu.VMEM((N_KV_GROUPS, AUG_ROWS, wide), F32),
            pltpu.VMEM((N_KV_GROUPS, BK_SEL, wide), F32),
            pltpu.VMEM((D_MODEL, TQ), F32),
            pltpu.VMEM((N_KV_GROUPS, N_SEL_P, TQ), F32),
            pltpu.VMEM((N_KV_GROUPS, 1, wide), F32),
            pltpu.VMEM((N_KV_GROUPS, 1, wide), F32),
            pltpu.VMEM((N_KV_GROUPS, HEAD_DIM, wide), F32),
        ],
        compiler_params=_cparams(2),
    )(q, gates, x, kc, vct, ksel, vselt, kwin, vwint, ovl_t, w_out)


def _nsa_sample_kernel(pt_ref, q_ref, g_ref, kc_ref, vct_ref, kvn_ref, wcol_ref, win_ref, *refs):
    n_tok = q_ref.shape[0]
    n_pg = n_tok * N_PAGES
    pages = [refs[j * N_PAGES:(j + 1) * N_PAGES] for j in range(n_tok)]
    slope_ref, ovl_ref, exp_ref, o_ref = refs[n_pg:n_pg + 4]
    wino_ref = refs[n_pg + 4] if len(refs) > n_pg + 4 else None
    toks = range(n_tok)
    rows = n_tok * N_HEADS
    b0 = pl.program_id(0) * n_tok
    t = PAST_LEN

    def stack(parts):
        return jnp.concatenate(parts, axis=0)

    def per_head(row):
        return jnp.broadcast_to(row, (N_HEADS, row.shape[1]))

    hrow = lax.broadcasted_iota(jnp.int32, (N_HEADS, GD), 0)
    col = lax.broadcasted_iota(jnp.int32, (N_HEADS, GD), 1)
    own1 = (col // HEAD_DIM) == (hrow // Q_PER_GROUP)
    own = stack([own1] * n_tok)
    qbd = []
    for j in toks:
        q = q_ref[j]
        qh = jnp.concatenate([q[:, h * HEAD_DIM:(h + 1) * HEAD_DIM] for h in range(N_HEADS)], axis=0)
        q4 = jnp.concatenate([qh] * N_KV_GROUPS, axis=1)
        qbd.append(jnp.where(own1, q4, 0.0).astype(BF16))
    qbd_f = stack(qbd).astype(F32)
    slope = stack([slope_ref[:, 0:1]] * n_tok)
    kvn = [kvn_ref[j] for j in toks]

    def new_key_score(lo):
        k_new = stack([per_head(kvn[j][:, lo:lo + GD]) for j in toks])
        return jnp.sum(qbd_f * k_new.astype(BF16).astype(F32), axis=1, keepdims=True)

    def new_val(p_new, lo):
        v_new = stack([per_head(kvn[j][:, lo:lo + GD]) for j in toks])
        return p_new.astype(BF16).astype(F32) * v_new.astype(BF16).astype(F32)

    win = [win_ref[j] for j in toks]
    sc_raw = stack([_dot_nt(qbd[j], kc_ref[j]) for j in toks])
    sw_raw = stack([_dot(qbd[j], win[j][0:GD, :].astype(BF16)) for j in toks])
    s_all = stack([jnp.concatenate([_dot(qbd[j], pg_ref[0, 0:GD, :].astype(BF16)) for pg_ref in pages[j]], axis=1)
                   for j in toks])

    i_w = lax.broadcasted_iota(jnp.int32, (1, WINDOW), 1)
    s_w = sw_raw - slope * (WINDOW - i_w).astype(F32)
    s_w = jnp.where(i_w >= 1, s_w, NEG_INF)
    s_wn = new_key_score(4 * GD)
    m = jnp.maximum(jnp.max(s_w, axis=1, keepdims=True), s_wn)
    p_w = jnp.exp(s_w - m)
    p_wn = jnp.exp(s_wn - m)
    l = jnp.sum(p_w, axis=1, keepdims=True) + p_wn
    p_wb = p_w.astype(BF16)
    pv_w = stack([_dot_nt(p_wb[j * N_HEADS:(j + 1) * N_HEADS, :], win[j][GD:, :].astype(BF16)) for j in toks])
    o_w = (new_val(p_wn, 5 * GD) + pv_w) / l

    if wino_ref is not None:
        lane_b = lax.broadcasted_iota(jnp.int32, wcol_ref.shape, 1)
        lane_w = lax.broadcasted_iota(jnp.int32, win[0].shape, 1)
        for j in toks:
            new_col = jnp.sum(jnp.where(lane_b == b0 + j, wcol_ref[...], 0.0), axis=1, keepdims=True)
            wino_ref[j] = jnp.where(lane_w == WINDOW - 1, new_col, pltpu.roll(win[j], WINDOW - 1, axis=1))

    lane_c = lax.broadcasted_iota(jnp.int32, (1, N_CMP_PAD), 1)
    d_c = t - (lane_c * CMP_STRIDE + CMP_LEN - 1)
    ok_c = d_c >= 0
    s = sc_raw - slope * d_c.astype(F32)
    s = jnp.where(ok_c, s, NEG_INF)
    p = jnp.exp(s - jnp.max(s, axis=1, keepdims=True))
    p = p / jnp.sum(p, axis=1, keepdims=True)
    p = jnp.where(ok_c, p, 0.0)
    p_b = p.astype(BF16)
    o_c = stack([_dot_nt(p_b[j * N_HEADS:(j + 1) * N_HEADS, :], vct_ref[j]) for j in toks])

    grp_r = lax.broadcasted_iota(jnp.int32, (rows, rows), 0) // Q_PER_GROUP
    grp_c = lax.broadcasted_iota(jnp.int32, (rows, rows), 1) // Q_PER_GROUP
    pg = _dot(jnp.where(grp_r == grp_c, 1.0, 0.0), p, precision=HIGHEST)
    imp = _dot(pg, ovl_ref[...], precision=HIGHEST)
    j_idx = lax.broadcasted_iota(jnp.int32, (1, LANES), 1)
    cur = t // SLC_BLOCK
    forced = (j_idx == 0) | (j_idx == cur) | (j_idx == cur - 1)
    score = jnp.where(j_idx <= cur, jnp.where(forced, FORCE_SCORE, imp), NEG_INF)
    selneg = _top_mask(score, cur + 1, j_idx, 1)
    sel = jnp.where(selneg == 0.0, 1.0, 0.0).astype(BF16)
    selexp = _dot(sel, exp_ref[...])

    pos = lax.broadcasted_iota(jnp.int32, (1, PAST_LEN), 1)
    s_all = jnp.where(selexp > 0.5, s_all - slope * (t - pos).astype(F32), NEG_INF)
    s_new = new_key_score(2 * GD)
    m = jnp.maximum(jnp.max(s_all, axis=1, keepdims=True), s_new)
    p_all = jnp.exp(s_all - m)
    p_new = jnp.exp(s_new - m)
    l = jnp.sum(p_all, axis=1, keepdims=True) + p_new
    p_all = p_all.astype(BF16)
    pv_s = []
    for j in toks:
        acc = None
        for i, pg_ref in enumerate(pages[j]):
            part = _dot_nt(p_all[j * N_HEADS:(j + 1) * N_HEADS, i * PAGE_SIZE:(i + 1) * PAGE_SIZE],
                           pg_ref[0, GD:, :].astype(BF16))
            acc = part if acc is None else acc + part
        pv_s.append(acc)
    o_s = (new_val(p_new, 3 * GD) + stack(pv_s)) / l

    grow = stack([per_head(g_ref[j]) for j in toks])
    h128 = lax.broadcasted_iota(jnp.int32, (rows, LANES), 0) % N_HEADS
    c128 = lax.broadcasted_iota(jnp.int32, (rows, LANES), 1)

    def gate(br):
        return jnp.sum(jnp.where(c128 == 3 * h128 + br, grow, 0.0), axis=1, keepdims=True)

    o = jnp.where(own, gate(0) * o_c + gate(1) * o_s + gate(2) * o_w, 0.0)
    oh = (o[:, 0:HEAD_DIM] + o[:, HEAD_DIM:2 * HEAD_DIM]
          + o[:, 2 * HEAD_DIM:3 * HEAD_DIM] + o[:, 3 * HEAD_DIM:4 * HEAD_DIM])
    for j in toks:
        o_ref[j] = jnp.concatenate([oh[j * N_HEADS + h:j * N_HEADS + h + 1, :] for h in range(N_HEADS)], axis=1)


def _nsa_sample(page_table, q_s, g_s, kc, vct, kvn, wcol, state_t, cache_t, slopes, ovl, expand, emit_window):
    n_tok = NB_S if emit_window else NB_S_LATE
    per_b = lambda *shape: pl.BlockSpec((n_tok,) + shape, lambda b, pt: (b,) + (0,) * len(shape))
    tok_page = lambda j, p: _page_spec(n_tok, j, p, 1)
    n_out = 2 if emit_window else 1
    return pl.pallas_call(
        _nsa_sample_kernel,
        grid_spec=pltpu.PrefetchScalarGridSpec(
            num_scalar_prefetch=1,
            grid=(DEC_BATCH // n_tok,),
            in_specs=[
                per_b(1, D_MODEL), per_b(1, LANES), per_b(N_CMP_PAD, GD), per_b(GD, N_CMP_PAD),
                per_b(1, KV_ROW + KV_WIN), _const_sp((KV_WIN, DEC_BATCH)), per_b(KV_WIN, WINDOW),
            ] + [tok_page(j, p) for j in range(n_tok) for p in range(N_PAGES)] + [
                _const_sp((N_HEADS, LANES)), _const_sp((N_CMP_PAD, LANES)), _const_sp((LANES, PAST_LEN)),
            ],
            out_specs=[per_b(1, D_MODEL), per_b(KV_WIN, WINDOW)][:n_out],
        ),
        out_shape=[
            jax.ShapeDtypeStruct((DEC_BATCH, 1, D_MODEL), F32),
            jax.ShapeDtypeStruct((DEC_BATCH, KV_WIN, WINDOW), F32),
        ][:n_out],
        compiler_params=_cparams(1),
    )(page_table, q_s, g_s, kc, vct, kvn, wcol, state_t, *([cache_t] * (n_tok * N_PAGES)), slopes, ovl, expand)


def _outproj_sample_kernel(o_ref, x_ref, w_ref, xo_ref):
    xo_ref[...] = x_ref[...] + _dot(o_ref[...].astype(BF16), w_ref[...])


def _outproj_sample(o_pad, x, w_out):
    return pl.pallas_call(
        _outproj_sample_kernel,
        grid=(1,),
        in_specs=[
            pl.BlockSpec((TM, D_MODEL), lambda i: (0, 0)),
            pl.BlockSpec((TM, D_MODEL), lambda i: (N_PT, 0)),
            _const((D_MODEL, D_MODEL)),
        ],
        out_specs=pl.BlockSpec((TM, D_MODEL), lambda i: (N_PT, 0)),
        out_shape=jax.ShapeDtypeStruct((N_TOT, D_MODEL), F32),
        input_output_aliases={1: 0},
        compiler_params=_cparams(1),
    )(o_pad, x, w_out)


def _final_norm_kernel(x_ref, nrm_ref, yp_ref, ys_ref):
    i = pl.program_id(0)
    y = _rms(x_ref[...], nrm_ref[...])

    @pl.when(i < N_PT)
    def _():
        yp_ref[...] = y

    @pl.when(i == N_PT)
    def _():
        ys_ref[...] = y[:DEC_BATCH, :]


def _final_norm(x, nrm):
    return pl.pallas_call(
        _final_norm_kernel,
        grid=(N_TILES,),
        in_specs=[pl.BlockSpec((TM, D_MODEL), lambda i: (i, 0)), _const((1, D_MODEL))],
        out_specs=[
            pl.BlockSpec((TM, D_MODEL), lambda i: (jnp.minimum(i, N_PT - 1), 0)),
            pl.BlockSpec((DEC_BATCH, D_MODEL), lambda i: (0, 0)),
        ],
        out_shape=[
            jax.ShapeDtypeStruct((N_P, D_MODEL), F32),
            jax.ShapeDtypeStruct((DEC_BATCH, D_MODEL), F32),
        ],
        compiler_params=_cparams(1),
    )(x, nrm)


def _row(v):
    return v.reshape(1, -1).astype(F32)


def _prep_mix(w_s, b_s):
    causal = jnp.tril(jnp.ones((CHUNK, CHUNK), F32))
    eye = jnp.eye(CHUNK, dtype=F32)
    w0 = w_s * causal
    w1 = w_s[:, 0:1, 0:1] * eye
    b0 = jnp.repeat(b_s.T, D_A // A_GROUPS, axis=1)
    b1 = jnp.broadcast_to(jnp.repeat(b_s[:, 0], D_A // A_GROUPS)[None, :], (CHUNK, D_A))
    return jnp.stack([w0, w1]).astype(BF16), jnp.stack([b0, b1]).astype(F32)


def _prep_compress(cmp_pe, cmp_w1, cmp_w2):
    eye = jnp.eye(N_KV_GROUPS, dtype=F32)
    w1 = cmp_w1.reshape(2, 2, CMP_STRIDE, HEAD_DIM, HEAD_DIM)
    ws = jnp.einsum('khrde,gj->krgdhje', w1, eye).reshape(2, CMP_STRIDE * GD, 2 * GD).astype(BF16)
    cvec = jnp.einsum('kld,klde->ke', cmp_pe, cmp_w1, precision=HIGHEST)
    cvec = jnp.tile(cvec, (1, N_KV_GROUPS)).reshape(2, 1, GD).astype(F32)
    w2bd = jnp.einsum('ked,gj->kgejd', cmp_w2, eye).reshape(2, GD, GD).astype(BF16)
    return ws, cvec, w2bd


def _overlap():
    n = np.arange(N_CMP_PAD)[:, None] * CMP_STRIDE
    s0 = np.arange(LANES)[None, :] * SLC_BLOCK
    ovl = ((n < s0 + SLC_BLOCK) & (n + CMP_LEN > s0)).astype(np.float32)
    ovl[N_CMP_PAD - 1, :] = 0.0
    return ovl


def _position_minor(a):
    return jnp.transpose(a, (0, 2, 3, 4, 1)).reshape(a.shape[0], -1, a.shape[1])


def _token_major(a, n_kinds):
    a = a.reshape(a.shape[0], n_kinds, N_KV_GROUPS, HEAD_DIM, a.shape[-1])
    return jnp.transpose(a, (0, 4, 1, 2, 3))


def kernel(x_prompt, x_sample, cache_kv, state_win_kv, page_table, norm_mix, norm_ffn, norm_final, a_w_in, a_ln_g, a_ln_b, a_w_s, a_b_s, a_w_out, kv_norm, w_kv, cmp_pe, cmp_w1, cmp_w2, b_w_in, b_b_gate, b_w_out, f_w_gate, f_w_up, f_w_down, m_w_router, m_b_router, m_w_gate, m_w_up, m_w_down):
    x = jnp.concatenate([x_prompt.reshape(N_P, D_MODEL), x_sample.reshape(DEC_BATCH, D_MODEL),
                         jnp.zeros((TM - DEC_BATCH, D_MODEL), F32)], axis=0)
    cache_t = _position_minor(cache_kv)
    state_t = _position_minor(state_win_kv)
    ovl = _overlap()
    ovl_s = jnp.asarray(ovl)
    ovl_pt = jnp.asarray(ovl[:, :N_SEL_P].T.copy())
    expand = jnp.asarray((np.arange(PAST_LEN)[None, :] // SLC_BLOCK == np.arange(LANES)[:, None])
                         .astype(np.float32)).astype(BF16)
    slopes = jnp.asarray(np.repeat(np.asarray(_SLOPES, np.float32)[:, None], LANES, axis=1))
    moe_w = [w.astype(BF16) for w in (m_w_gate, m_w_up, m_w_down)]

    v_p, v_s = [], []
    win_s = None
    for layer in range(DEPTH):
        if layer == N_A_LAYERS:
            wkv = w_kv.astype(BF16)
            (rowst_p, wint_p, kvt_s, kvn_s, cmp_nat,
             ksel, kwin, vselt, vwint) = _kvproj(x, _row(kv_norm), wkv, wkv.T)
            ws, cvec, w2bd = _prep_compress(cmp_pe, cmp_w1, cmp_w2)
            kc_p, vct_p = _compress_prompt(cmp_nat, ws, cvec, w2bd)
            kc_s, vct_s = _compress_sample(page_table, cache_t, ws, cvec, w2bd)
        if layer < N_A_LAYERS:
            wmix, bias = _prep_mix(a_w_s[layer], a_b_s[layer])
            x, vp, vs = _a_mixer(x, _row(norm_mix[layer]), a_w_in[layer].astype(BF16), _row(a_ln_g[layer]),
                                 _row(a_ln_b[layer]), wmix, bias, a_w_out[layer].astype(BF16))
            v_p.append(vp)
            v_s.append(vs)
        else:
            i = layer - N_A_LAYERS
            w_in = jnp.pad(b_w_in[i], ((0, 0), (0, LANES - 3 * N_HEADS))).astype(BF16)
            bg = jnp.pad(b_b_gate[i], (0, LANES - 3 * N_HEADS)).reshape(1, LANES).astype(F32)
            w_out = b_w_out[i].astype(BF16)
            q, gates = _qproj(x, _row(norm_mix[layer]), w_in, bg)
            q_s = q[N_P:N_P + DEC_BATCH].reshape(DEC_BATCH, 1, D_MODEL)
            g_s = gates[N_P:N_P + DEC_BATCH].reshape(DEC_BATCH, 1, LANES)
            outs = _nsa_sample(page_table, q_s, g_s, kc_s, vct_s,
                               kvn_s.reshape(DEC_BATCH, 1, KV_ROW + KV_WIN), kvt_s[KV_ROW:],
                               state_t, cache_t, slopes, ovl_s, expand, emit_window=(i == 0))
            o_s = outs[0]
            if i == 0:
                win_s = outs[1]
            x = _nsa_prompt(q, gates, x, kc_p, vct_p, ksel, vselt, kwin, vwint, ovl_pt, w_out)
            o_pad = jnp.pad(o_s.reshape(DEC_BATCH, D_MODEL), ((0, TM - DEC_BATCH), (0, 0)))
            x = _outproj_sample(o_pad, x, w_out)
        j = layer // 2
        if layer % 2 == 0:
            x = _ffn(x, _row(norm_ffn[layer]), f_w_gate[j].astype(BF16), f_w_up[j].astype(BF16),
                     f_w_down[j].astype(BF16))
        else:
            wr = jnp.pad(m_w_router[j], ((0, 0), (0, LANES - N_EXPERTS))).astype(F32)
            br = jnp.pad(m_b_router[j], (0, LANES - N_EXPERTS), constant_values=NEG_INF).reshape(1, LANES)
            x = _moe_layer(x, _row(norm_ffn[layer]), wr, br.astype(F32), *moe_w, j)
    y_p, y_s = _final_norm(x, _row(norm_final))

    return (y_p.reshape(BATCH, SEQ, D_MODEL),
            y_s.reshape(DEC_BATCH, 1, D_MODEL),
            _token_major(rowst_p, 4),
            _token_major(kvt_s[:KV_ROW].reshape(1, KV_ROW, DEC_BATCH), 4).reshape(
                DEC_BATCH, 1, 4, N_KV_GROUPS, HEAD_DIM),
            _token_major(wint_p[:, :, SEQ - WINDOW:], 2),
            _token_major(win_s, 2),
            jnp.stack(v_p),
            jnp.stack(v_s).reshape(N_A_LAYERS, DEC_BATCH, 1, D_A))
```
